```python
import jax, jax.numpy as jnp
from jax import lax
import numpy as np

D_MODEL = 2048
BATCH = 1
SEQ = 8192
DEPTH = 1
DEC_BATCH = 32
DEC_SEQ = 8
PAST_LEN = 8192
PAGE_SIZE = 128

HEAD_DIM = 128
N_ATT_HEADS = 8
D_ATT = N_ATT_HEADS * HEAD_DIM
D_CONV = D_MODEL - D_ATT
CONV_WIDTH = 3
DILATED_BRANCHES = ((128, 1), (512, 4), (2048, 16))
MAX_WINDOW = 2048
N_BUCKETS = 32
REL_MAX_DISTANCE = 2048
D_FF = ((8 * D_MODEL + 767) // 768) * 256
D_IN = 3 * D_CONV + 3 * D_ATT
QBLOCK = 128
EPS = 1e-6

kernel_name = "hybrid_shortconv_dilated_swa_step"


def _branch_distances_and_buckets():
    dist = np.stack([np.arange(0, w + 1, d) for (w, d) in DILATED_BRANCHES]).astype(np.int64)
    max_exact = N_BUCKETS // 2
    large = max_exact + (np.log(np.maximum(dist, 1) / max_exact)
                         / np.log(REL_MAX_DISTANCE / max_exact)
                         * (N_BUCKETS - max_exact)).astype(np.int64)
    large = np.minimum(large, N_BUCKETS - 1)
    bucket = np.where(dist < max_exact, dist, large)
    return dist.astype(np.int32), bucket.astype(np.int32)


def rms_norm(x, g):
    xf = x.astype(jnp.float32)
    y = xf * lax.rsqrt(jnp.mean(xf * xf, axis=-1, keepdims=True) + EPS)
    return (y * g.astype(jnp.float32)).astype(x.dtype)


def dilated_attention(q, k_ext, v_ext, qpos, dist, bias):
    idx = qpos[:, None, None] - dist[None]
    valid = idx >= 0
    idx = jnp.maximum(idx, 0)
    kg = jnp.take(k_ext, idx, axis=1, mode="clip")
    vg = jnp.take(v_ext, idx, axis=1, mode="clip")
    logits = jnp.einsum('bqhd,bqgnhd->bqhgn', q, kg).astype(jnp.float32) * (HEAD_DIM ** -0.5)
    logits = logits + bias[None, None]
    logits = jnp.where(valid[None, :, None], logits, -1e30)
    m = jnp.max(logits, axis=-1, keepdims=True)
    p = jnp.exp(logits - m)
    s = jnp.sum(p, axis=-1, keepdims=True)
    o = jnp.einsum('bqhgn,bqgnhd->bqhgd', p, vg.astype(jnp.float32)) / s
    w = jax.nn.softmax((m + jnp.log(s))[..., 0], axis=-1)
    out = jnp.einsum('bqhg,bqhgd->bqhd', w, o)
    return out.astype(q.dtype)


def prompt_attention(q, k, v, dist, bias):
    B, S, H, D = q.shape
    nb = S // QBLOCK
    qb = q.reshape(B, nb, QBLOCK, H, D).transpose(1, 0, 2, 3, 4)
    starts = jnp.arange(nb, dtype=jnp.int32) * QBLOCK

    def block(args):
        qi, st = args
        return dilated_attention(qi, k, v, st + jnp.arange(QBLOCK, dtype=jnp.int32), dist, bias)

    out = lax.map(block, (qb, starts))
    return out.transpose(1, 0, 2, 3, 4).reshape(B, S, H, D)


def layer_forward(x, conv_prev, k_prev, v_prev, norm_mix, w_in, conv_w, q_norm, k_norm,
                  w_out, norm_ffn, w_gate, w_up, w_down, dist, bias):
    B, T, _ = x.shape
    h = rms_norm(x, norm_mix)
    proj = h @ w_in
    xc, bg, cg, q, k, v = jnp.split(
        proj, np.cumsum([D_CONV, D_CONV, D_CONV, D_ATT, D_ATT]).tolist(), axis=-1)
    u = cg * xc
    u_ext = jnp.concatenate([conv_prev, u], axis=1)
    conv = u_ext[:, 0:T] * conv_w[0]
    for j in range(1, CONV_WIDTH):
        conv = conv + u_ext[:, j:j + T] * conv_w[j]
    y_conv = bg * conv
    q = rms_norm(q.reshape(B, T, N_ATT_HEADS, HEAD_DIM), q_norm)
    k = rms_norm(k.reshape(B, T, N_ATT_HEADS, HEAD_DIM), k_norm)
    v = v.reshape(B, T, N_ATT_HEADS, HEAD_DIM)
    if k_prev is None:
        att = prompt_attention(q, k, v, dist, bias)
    else:
        k_ext = jnp.concatenate([k_prev, k], axis=1)
        v_ext = jnp.concatenate([v_prev, v], axis=1)
        qpos = k_prev.shape[1] + jnp.arange(T, dtype=jnp.int32)
        att = dilated_attention(q, k_ext, v_ext, qpos, dist, bias)
    mix = jnp.concatenate([y_conv, att.reshape(B, T, D_ATT)], axis=-1) @ w_out
    x = x + mix
    f = rms_norm(x, norm_ffn)
    x = x + (jax.nn.silu(f @ w_gate) * (f @ w_up)) @ w_down
    new_conv = u_ext[:, -(CONV_WIDTH - 1):]
    return x, new_conv, k, v


def setup_inputs(seed: int = 0) -> dict:
    key = jax.random.key(seed)
    ks = jax.random.split(key, 20)
    w_buf = min(MAX_WINDOW, PAST_LEN)
    f32 = jnp.float32
    nrm = lambda k, shape, s: jax.random.normal(k, shape, f32) * s
    return {
        "x_prompt": nrm(ks[0], (BATCH, SEQ, D_MODEL), 1.0),
        "x_sample": nrm(ks[1], (DEC_BATCH, DEC_SEQ, D_MODEL), 1.0),
        "cache_k": nrm(ks[2], (DEPTH, DEC_BATCH, w_buf, N_ATT_HEADS, HEAD_DIM), 1.0),
        "cache_v": nrm(ks[3], (DEPTH, DEC_BATCH, w_buf, N_ATT_HEADS, HEAD_DIM), 1.0),
        "state_conv": nrm(ks[4], (DEPTH, DEC_BATCH, CONV_WIDTH - 1, D_CONV), 1.0),
        "norm_mix": 1.0 + nrm(ks[5], (DEPTH, D_MODEL), 0.02),
        "w_in": nrm(ks[6], (DEPTH, D_MODEL, D_IN), D_MODEL ** -0.5),
        "conv_w": nrm(ks[7], (DEPTH, CONV_WIDTH, D_CONV), CONV_WIDTH ** -0.5),
        "q_norm": 1.0 + nrm(ks[8], (DEPTH, HEAD_DIM), 0.02),
        "k_norm": 1.0 + nrm(ks[9], (DEPTH, HEAD_DIM), 0.02),
        "rel_bias": nrm(ks[10], (N_BUCKETS, N_ATT_HEADS), 0.1),
        "w_out": nrm(ks[11], (DEPTH, D_MODEL, D_MODEL), D_MODEL ** -0.5),
        "norm_ffn": 1.0 + nrm(ks[12], (DEPTH, D_MODEL), 0.02),
        "w_gate": nrm(ks[13], (DEPTH, D_MODEL, D_FF), D_MODEL ** -0.5),
        "w_up": nrm(ks[14], (DEPTH, D_MODEL, D_FF), D_MODEL ** -0.5),
        "w_down": nrm(ks[15], (DEPTH, D_FF, D_MODEL), D_FF ** -0.5),
    }


def reference(x_prompt, x_sample, cache_k, cache_v, state_conv, norm_mix, w_in, conv_w,
              q_norm, k_norm, rel_bias, w_out, norm_ffn, w_gate, w_up, w_down):
    dist_np, bucket_np = _branch_distances_and_buckets()
    dist = jnp.asarray(dist_np)
    bias = rel_bias.astype(jnp.float32)[jnp.asarray(bucket_np)].transpose(2, 0, 1)
    prompt_keep = min(MAX_WINDOW, x_prompt.shape[1])
    xp, xs = x_prompt, x_sample
    pk, pv, pc, sk, sv, sc = [], [], [], [], [], []
    for l in range(DEPTH):
        lw = (norm_mix[l], w_in[l], conv_w[l], q_norm[l], k_norm[l], w_out[l], norm_ffn[l],
              w_gate[l], w_up[l], w_down[l], dist, bias)
        conv0 = jnp.zeros((xp.shape[0], CONV_WIDTH - 1, D_CONV), xp.dtype)
        xp, c_p, k_p, v_p = layer_forward(xp, conv0, None, None, *lw)
        xs, c_s, k_s, v_s = layer_forward(xs, state_conv[l], cache_k[l], cache_v[l], *lw)
        pk.append(k_p[:, -prompt_keep:])
        pv.append(v_p[:, -prompt_keep:])
        pc.append(c_p)
        sk.append(k_s)
        sv.append(v_s)
        sc.append(c_s)
    return (xp, xs, jnp.stack(pk), jnp.stack(pv), jnp.stack(pc),
            jnp.stack(sk), jnp.stack(sv), jnp.stack(sc))
```

```python
import functools

import numpy as np
import jax
import jax.numpy as jnp
from jax import lax
from jax.experimental import pallas as pl
from jax.experimental.pallas import tpu as pltpu

D_MODEL = 2048
HEAD_DIM = 128
N_HEADS = 8
D_ATT = N_HEADS * HEAD_DIM
D_CONV = D_MODEL - D_ATT
CONV_WIDTH = 3
BRANCHES = ((128, 1), (512, 4), (2048, 16))
N_BRANCH = len(BRANCHES)
assert tuple(d for _, d in BRANCHES) == (1, 4, 16)
BAND = 128
MAX_WINDOW = 2048
N_BUCKETS = 32
REL_MAX_DISTANCE = 2048
D_FF = ((8 * D_MODEL + 767) // 768) * 256
D_IN = 3 * D_CONV + 3 * D_ATT
EPS = 1e-6
NEG = -1e30
LOG2_E = float(np.log2(np.e))

F32 = jnp.float32
BF16 = jnp.bfloat16

VMEM_LIMIT = 56 * 1024 * 1024
SUBLANES = 8
BF16_SUBLANES = 16

QB = 128
QSTEP = 256
SAMPLE_SEQS_PER_STEP = 2
TM_IN = 512
TM_OUT = 512
TM_FFN = 1024
TF_FFN = 512
TF_FFN_SINGLE_TILE = D_FF // 4
NCHUNK = 512
NORM_ROWS = 64
HEADS_PER_CHUNK = NCHUNK // HEAD_DIM
N_CONV_CHUNKS = D_CONV // NCHUNK
N_ATT_CHUNKS = D_ATT // NCHUNK
NT_DIMS = (((1,), (1,)), ((), ()))


def _bucket_of_distance(dist):
    dist = np.asarray(dist, np.int64)
    max_exact = N_BUCKETS // 2
    large = max_exact + (np.log(np.maximum(dist, 1) / max_exact)
                         / np.log(REL_MAX_DISTANCE / max_exact)
                         * (N_BUCKETS - max_exact)).astype(np.int64)
    large = np.minimum(large, N_BUCKETS - 1)
    return np.where(dist < max_exact, dist, large).astype(np.int32)


def _multiplicity(dist):
    dist = np.asarray(dist, np.int64)
    m = np.zeros(dist.shape, np.int64)
    for w, d in BRANCHES:
        m += ((dist >= 0) & (dist <= w) & (dist % d == 0)).astype(np.int64)
    return m


def _bias_at(rel_bias, dist):
    dist = np.asarray(dist)
    onehot = np.zeros((N_BUCKETS, dist.size), np.float32)
    onehot[_bucket_of_distance(dist).ravel(), np.arange(dist.size)] = 1.0
    vals = jnp.dot(rel_bias.astype(F32).T * LOG2_E, onehot, precision=lax.Precision.HIGHEST)
    return vals.reshape((N_HEADS,) + dist.shape)


def _cparams(n_axes):
    return pltpu.CompilerParams(dimension_semantics=("arbitrary",) * n_axes,
                                vmem_limit_bytes=VMEM_LIMIT)


def _resident(shape):
    nd = len(shape)
    return pl.BlockSpec(shape, lambda *_: (0,) * nd, pipeline_mode=pl.Buffered(1))


def _inproj_kernel(*refs, tm, sample):
    x_ref, g_ref, wa_ref, wb_ref, wc_ref, cw_ref, qg_ref, kg_ref = refs[:8]
    n_in, n_layouts = (10, 1) if sample else (8, N_BRANCH)
    if sample:
        p1_ref, p2_ref = refs[8:10]
    outs = list(refs[n_in:])
    y_ref = outs.pop(0)
    qkv_out = [[outs.pop(0) for _ in range(n_layouts)] for _ in range(3)]
    kf_ref, vf_ref, u_ref, h_scr, u_scr = outs[:5]
    if not sample:
        carry_scr, slab_scr = outs[5:]
    i = pl.program_id(0)
    j = pl.program_id(1)

    @pl.when(j == 0)
    def _():
        def norm_rows(r, carry):
            rows = pl.ds(pl.multiple_of(r * NORM_ROWS, NORM_ROWS), NORM_ROWS)
            xs = x_ref[rows, :]
            ms = jnp.mean(xs * xs, axis=-1, keepdims=True)
            h_scr[rows, :] = ((xs * lax.rsqrt(ms + EPS)) * g_ref[...]).astype(BF16)
            return carry
        lax.fori_loop(0, tm // NORM_ROWS, norm_rows, 0, unroll=2)

    if not sample:
        @pl.when((i == 0) & (j == 0))
        def _():
            carry_scr[...] = jnp.zeros_like(carry_scr)

    def proj(w_ref):
        return jnp.dot(h_scr[...], w_ref[...], preferred_element_type=F32)

    def conv_chunk(c):
        cs = slice(c * NCHUNK, (c + 1) * NCHUNK)
        u = proj(wc_ref) * proj(wa_ref)
        if sample:
            u_scr[0:SUBLANES, :] = jnp.zeros((SUBLANES, NCHUNK), F32)
        else:
            u_scr[0:SUBLANES, :] = carry_scr[:, cs]
        u_scr[SUBLANES:SUBLANES + tm, :] = u
        u_m1 = u_scr[SUBLANES - 1:SUBLANES - 1 + tm, :]
        u_m2 = u_scr[SUBLANES - 2:SUBLANES - 2 + tm, :]
        if sample:
            t_in_seq = lax.broadcasted_iota(jnp.int32, (tm, NCHUNK), 0) % SUBLANES
            u_m1 = jnp.where(t_in_seq == 0, p1_ref[...], u_m1)
            u_m2 = jnp.where(t_in_seq < 2, p2_ref[...], u_m2)
            u_ref[...] = u
        else:
            carry_scr[:, cs] = u[tm - SUBLANES:tm, :]
            u_ref[...] = u[tm - SUBLANES:tm, :]
        conv = u_m2 * cw_ref[0:1, :]
        conv = conv + u_m1 * cw_ref[1:2, :]
        conv = conv + u * cw_ref[2:3, :]
        y_ref[...] = (proj(wb_ref) * conv).astype(y_ref.dtype)

    def head_norm(z, gain):
        ms = jnp.mean(z * z, axis=-1, keepdims=True)
        return (z * lax.rsqrt(ms + EPS)) * gain

    def qkv_chunk(c):
        zq, zk, zv = proj(wa_ref), proj(wb_ref), proj(wc_ref)
        for hh in range(HEADS_PER_CHUNK):
            sl = slice(hh * HEAD_DIM, (hh + 1) * HEAD_DIM)
            kn = head_norm(zk[:, sl], kg_ref[...])
            vals = (head_norm(zq[:, sl], qg_ref[...]), kn, zv[:, sl])
            col = slice(c * NCHUNK + hh * HEAD_DIM, c * NCHUNK + (hh + 1) * HEAD_DIM)
            kf_ref[:, col] = kn
            vf_ref[:, col] = vals[2]
            for t_idx, (val, outs) in enumerate(zip(vals, qkv_out)):
                outs[0][:, sl] = val.astype(outs[0].dtype)
                if sample:
                    continue
                slab = slab_scr.at[t_idx * HEADS_PER_CHUNK + hh]
                slab[0] = val
                quarter = tm // 4
                for r4 in range(4):
                    cls4 = slab[0, pl.ds(r4, quarter, stride=4), :]
                    outs[1][r4, :, sl] = cls4.astype(outs[1].dtype)
                    slab[1, r4 * quarter:(r4 + 1) * quarter, :] = cls4
                for r4 in range(4):
                    for s in range(4):
                        cls16 = slab[1, pl.ds(r4 * quarter + s, tm // 16, stride=4), :]
                        outs[2][r4 + 4 * s, :, sl] = cls16.astype(outs[2].dtype)

    for c in range(N_CONV_CHUNKS):
        pl.when(j == c)(functools.partial(conv_chunk, c))
    for c in range(N_ATT_CHUNKS):
        pl.when(j == N_CONV_CHUNKS + c)(functools.partial(qkv_chunk, c))


def _inproj(x, gain, w_in, conv_w, q_gain, k_gain, prev=None):
    m = x.shape[0]
    sample = prev is not None
    tm = m if sample else TM_IN
    n_tiles = m // tm
    n_steps = N_CONV_CHUNKS + N_ATT_CHUNKS

    conv_col = lambda j: jnp.minimum(j, N_CONV_CHUNKS - 1)
    att_col = lambda j: jnp.maximum(j - N_CONV_CHUNKS, 0)
    def w_spec(t):
        def index(i, j):
            conv_blk = t * N_CONV_CHUNKS + j
            att_blk = 3 * N_CONV_CHUNKS + t * N_ATT_CHUNKS + (j - N_CONV_CHUNKS)
            return 0, jnp.where(j < N_CONV_CHUNKS, conv_blk, att_blk)
        return pl.BlockSpec((D_MODEL, NCHUNK), index)

    x_rows = pl.BlockSpec((tm, D_MODEL),
                          lambda i, j: (jnp.minimum(i + (j > 1).astype(jnp.int32), n_tiles - 1), 0))
    in_specs = [x_rows,
                pl.BlockSpec((1, D_MODEL), lambda i, j: (0, 0)),
                w_spec(0), w_spec(1), w_spec(2),
                pl.BlockSpec((CONV_WIDTH, NCHUNK), lambda i, j: (0, conv_col(j))),
                pl.BlockSpec((1, HEAD_DIM), lambda i, j: (0, 0)),
                pl.BlockSpec((1, HEAD_DIM), lambda i, j: (0, 0))]
    args = [x, gain, w_in, w_in, w_in, conv_w, q_gain, k_gain]
    conv_rows = pl.BlockSpec((tm, NCHUNK), lambda i, j: (i, conv_col(j)))
    att_rows = pl.BlockSpec((tm, NCHUNK), lambda i, j: (i, att_col(j)))
    win_rows = m if sample else min(MAX_WINDOW, m)
    win0 = n_tiles - win_rows // tm
    win = pl.BlockSpec((tm, D_ATT), lambda i, j: (jnp.maximum(i - win0, 0), 0))
    win_f32 = jax.ShapeDtypeStruct((win_rows, D_ATT), F32)
    scratch = [pltpu.VMEM((tm, D_MODEL), BF16), pltpu.VMEM((tm + SUBLANES, NCHUNK), F32)]
    if sample:
        in_specs += [conv_rows, conv_rows]
        args += list(prev)
        qkv_specs = [att_rows] * 3
        qkv_shapes = [jax.ShapeDtypeStruct((m, D_ATT), F32)] * 3
        u_spec, u_rows = conv_rows, m
    else:
        qkv_specs, qkv_shapes = [], []
        for _ in range(3):
            for _, d in BRANCHES:
                qkv_specs.append(pl.BlockSpec((None if d == 1 else d, tm // d, NCHUNK),
                                              lambda i, j: (0, i, att_col(j))))
                qkv_shapes.append(jax.ShapeDtypeStruct((d, m // d, D_ATT), BF16))
        u_spec = pl.BlockSpec((SUBLANES, NCHUNK), lambda i, j: (i, conv_col(j)))
        u_rows = n_tiles * SUBLANES
        scratch += [pltpu.VMEM((SUBLANES, D_CONV), F32),
                    pltpu.VMEM((3 * HEADS_PER_CHUNK, 2, tm, HEAD_DIM), F32)]
    out_specs = [conv_rows] + qkv_specs + [win, win, u_spec]
    out_shape = ([jax.ShapeDtypeStruct((m, D_CONV), BF16)] + qkv_shapes
                 + [win_f32, win_f32, jax.ShapeDtypeStruct((u_rows, D_CONV), F32)])
    return pl.pallas_call(
        functools.partial(_inproj_kernel, tm=tm, sample=sample),
        grid=(n_tiles, n_steps), in_specs=in_specs, out_specs=out_specs, out_shape=out_shape,
        scratch_shapes=scratch, compiler_params=_cparams(2),
        name="inproj_sample" if sample else "inproj_prompt",
    )(*args)


def _prompt_attn_kernel(*refs, steps_per_class, cast_every):
    n_cast = len(cast_every)
    base_ref = refs[0]
    ins = refs[1:1 + 5 * N_BRANCH]
    cast_in = refs[1 + 5 * N_BRANCH:1 + 5 * N_BRANCH + n_cast]
    outs = refs[1 + 5 * N_BRANCH + n_cast:-3]
    cast_out = outs[2 * N_BRANCH:]
    k_buf, v_buf, bias_ref = refs[-3:]
    i = pl.program_id(0)

    @pl.when(i == 0)
    def _():
        left = lax.broadcasted_iota(jnp.int32, (QB, 2 * QB), 1) < QB
        for h in range(N_HEADS):
            for g in range(N_BRANCH):
                row = jnp.broadcast_to(base_ref[h, g:g + 1, :], (QB, 2 * QB))
                tile = pltpu.roll(row, 0, 1, stride=1, stride_axis=0)
                bias_ref[h, 0, g] = tile
                bias_ref[h, 1, g] = jnp.where(left, NEG, tile)
    for src, dst, every in zip(cast_in, cast_out, cast_every):
        @pl.when(i % every == 0)
        def _(src=src, dst=dst):
            dst[...] = src[...].astype(dst.dtype)
    for g in range(N_BRANCH):
        _, kp_ref, kc_ref, vp_ref, vc_ref = ins[5 * g:5 * g + 5]
        for buf, p_ref, c_ref in ((k_buf, kp_ref, kc_ref), (v_buf, vp_ref, vc_ref)):
            buf[g, 0:QB, :] = p_ref[...]
            buf[g, QB:QB + QSTEP, :] = c_ref[...]
    lane = lax.broadcasted_iota(jnp.int32, (QB, HEAD_DIM), 1)

    def sub_block(j, carry):
        r0 = pl.multiple_of(j * QB, QB)
        for g in range(N_BRANCH):
            q_ref = ins[5 * g]
            o_ref, lse_ref = outs[2 * g:2 * g + 2]
            first = ((i % steps_per_class[g] == 0) & (j == 0)).astype(jnp.int32)
            lse_tile = jnp.zeros((QB, HEAD_DIM), F32)
            for h in range(N_HEADS):
                sl = slice(h * HEAD_DIM, (h + 1) * HEAD_DIM)
                k2 = k_buf[g, pl.ds(r0, 2 * QB), sl]
                v2 = v_buf[g, pl.ds(r0, 2 * QB), sl]
                s = lax.dot_general(q_ref[pl.ds(r0, QB), sl], k2, NT_DIMS,
                                    preferred_element_type=F32)
                s = s + bias_ref[h, first, g]
                m = jnp.max(s, axis=-1, keepdims=True)
                p = jnp.exp2(s - m)
                l = jnp.sum(p, axis=-1, keepdims=True)
                o = jnp.dot(p.astype(BF16), v2, preferred_element_type=F32)
                o_ref[pl.ds(r0, QB), sl] = (o / l).astype(o_ref.dtype)
                lse_tile = jnp.where(lane == h, m + jnp.log2(l), lse_tile)
            lse_ref[pl.ds(r0, QB), :] = lse_tile
        return carry

    lax.fori_loop(0, QSTEP // QB, sub_block, 0)


def _prompt_attention(qkv, band_bias, to_bf16):
    s = qkv[0][0].shape[1]
    n_blocks = s // QSTEP
    sub = QSTEP // QB
    in_specs = [_resident(band_bias.shape)]
    args = [band_bias]
    out_specs, out_shape, spc_all = [], [], []
    for g, (_, d) in enumerate(BRANCHES):
        spc = s // d // QSTEP
        spc_all.append(spc)
        cur = lambda i, spc=spc: (i // spc, i % spc, 0)
        prev = lambda i, spc=spc: (i // spc, jnp.maximum((i % spc) * sub - 1, 0), 0)
        blk = lambda f, cols=D_ATT: pl.BlockSpec((None, QB if f is prev else QSTEP, cols), f)
        in_specs += [blk(cur), blk(prev), blk(cur), blk(prev), blk(cur)]
        args += [qkv[0][g], qkv[1][g], qkv[1][g], qkv[2][g], qkv[2][g]]
        out_specs += [blk(cur), blk(cur, HEAD_DIM)]
        out_shape += [jax.ShapeDtypeStruct((d, s // d, D_ATT), BF16),
                      jax.ShapeDtypeStruct((d, s // d, HEAD_DIM), F32)]
    cast_every = []
    for w in to_bf16:
        rows = w.shape[0]
        every = next(e for e in (1, 2, 4, 8) if rows * e % (n_blocks * BF16_SUBLANES) == 0)
        cast_every.append(every)
        slab = pl.BlockSpec((rows * every // n_blocks, w.shape[1]),
                            lambda i, every=every: (i // every, 0))
        in_specs.append(slab)
        args.append(w)
        out_specs.append(slab)
        out_shape.append(jax.ShapeDtypeStruct(w.shape, BF16))
    outs = pl.pallas_call(
        functools.partial(_prompt_attn_kernel, steps_per_class=tuple(spc_all),
                          cast_every=tuple(cast_every)),
        grid=(n_blocks,), in_specs=in_specs, out_specs=out_specs, out_shape=out_shape,
        scratch_shapes=[pltpu.VMEM((N_BRANCH, QB + QSTEP, D_ATT), BF16)] * 2
                       + [pltpu.VMEM((N_HEADS, 2, N_BRANCH, QB, 2 * QB), F32)],
        compiler_params=_cparams(1), name="prompt_attention",
    )(*args)
    att = outs[:2 * N_BRANCH]
    return (att[0::2], att[1::2]), outs[2 * N_BRANCH:]


def _band_bias_rows(rel_bias):
    jj = np.arange(2 * QB)
    n = QB - jj
    inside = (n >= 0) & (n <= BAND)
    dist = np.stack([np.clip(n, 0, BAND) * d for _, d in BRANCHES])
    return jnp.where(inside, _bias_at(rel_bias, dist), NEG)


def _sample_attn_kernel(q_ref, kn_ref, vn_ref, ks_ref, kd_ref, vs_ref, vd_ref,
                        bs_ref, ms_ref, xs_ref, bd_ref, md_ref, xd_ref, bn_ref, mn_ref, xn_ref,
                        o_ref):
    n_seq = ks_ref.shape[0]
    t = q_ref.shape[0] // n_seq

    def logits(qk, b_ref, m_ref):
        n = qk.shape[-1]
        return ((qk.reshape(N_HEADS, t, n) + b_ref[...][None]).reshape(N_HEADS * t, n)
                + m_ref[...])

    for b in range(n_seq):
        rows = slice(b * t, (b + 1) * t)

        def head_major(ref):
            return jnp.concatenate([ref[rows, h * HEAD_DIM:(h + 1) * HEAD_DIM]
                                    for h in range(N_HEADS)], axis=0)

        def flat(ref):
            return ref[b].reshape(-1, HEAD_DIM).astype(BF16)

        q = head_major(q_ref)
        qb = q.astype(BF16)
        s = [logits(lax.dot_general(qb, flat(ks_ref), NT_DIMS, preferred_element_type=F32),
                    bs_ref, ms_ref),
             logits(lax.dot_general(qb, flat(kd_ref), NT_DIMS, preferred_element_type=F32),
                    bd_ref, md_ref),
             logits(lax.dot_general(q, head_major(kn_ref), NT_DIMS, preferred_element_type=F32),
                    bn_ref, mn_ref)]
        m = functools.reduce(jnp.maximum, [jnp.max(v, axis=-1, keepdims=True) for v in s])
        p = [jnp.exp2(v - m) * mult[...] for v, mult in zip(s, (xs_ref, xd_ref, xn_ref))]
        l = sum(jnp.sum(v, axis=-1, keepdims=True) for v in p)
        acc = jnp.dot(p[0].astype(BF16), flat(vs_ref), preferred_element_type=F32)
        acc = acc + jnp.dot(p[1].astype(BF16), flat(vd_ref), preferred_element_type=F32)
        acc = acc + jnp.dot(p[2], head_major(vn_ref), preferred_element_type=F32)
        acc = acc / l
        for h in range(N_HEADS):
            o_ref[rows, h * HEAD_DIM:(h + 1) * HEAD_DIM] = acc[h * t:(h + 1) * t, :]


def _sample_attention(q, k_new, v_new, cache_k, cache_v, rel_bias, t):
    b, w = cache_k.shape[:2]
    grp = BRANCHES[-1][1]
    dense = BRANCHES[-2][0]
    n_grp, sparse_grps, dense_grps = w // grp, (w - dense) // grp, dense // grp
    assert w % grp == 0 and dense % grp == 0 and t <= grp and sparse_grps % dense_grps == 0
    pos = np.arange(w).reshape(n_grp, grp)
    pos_sparse, pos_dense = pos[:sparse_grps, :t].ravel(), pos[sparse_grps:].ravel()
    qpos = w + np.arange(t)[:, None]
    assert not _multiplicity(qpos - pos[:sparse_grps, t:].reshape(1, -1)).any()

    def tables(dist, key_major):
        mult = _multiplicity(dist)
        bias = _bias_at(rel_bias, np.clip(dist, 0, REL_MAX_DISTANCE))
        ok = (mult > 0)[None, :, :, None] & np.eye(N_HEADS, dtype=bool)[:, None, None, :]
        mask = np.where(ok, 0.0, NEG).astype(np.float32)
        mult = np.where(ok, mult[None, :, :, None], 0).astype(np.float32)
        if key_major:
            bias = bias.transpose(1, 2, 0)
        else:
            bias = bias.transpose(1, 0, 2)
            mask, mult = mask.transpose(0, 1, 3, 2), mult.transpose(0, 1, 3, 2)
        rows = N_HEADS * t
        return (bias.reshape(t, -1), jnp.asarray(mask.reshape(rows, -1)),
                jnp.asarray(mult.reshape(rows, -1)))

    tabs = (tables(qpos - pos_sparse[None, :], True) + tables(qpos - pos_dense[None, :], True)
            + tables(qpos - qpos.T, False))

    view = lambda c: c.reshape(b, n_grp, grp * N_HEADS, HEAD_DIM)
    n_seq = SAMPLE_SEQS_PER_STEP
    assert b % n_seq == 0
    new = pl.BlockSpec((n_seq * t, D_ATT), lambda i: (i, 0))
    sparse = pl.BlockSpec((n_seq, sparse_grps, t * N_HEADS, HEAD_DIM), lambda i: (i, 0, 0, 0))
    dense_spec = pl.BlockSpec((n_seq, dense_grps, grp * N_HEADS, HEAD_DIM),
                              lambda i: (i, sparse_grps // dense_grps, 0, 0))
    return pl.pallas_call(
        _sample_attn_kernel,
        grid=(b // n_seq,),
        in_specs=[new, new, new, sparse, dense_spec, sparse, dense_spec]
                 + [_resident(a.shape) for a in tabs],
        out_specs=new,
        out_shape=jax.ShapeDtypeStruct((b * t, D_ATT), F32),
        compiler_params=_cparams(1), name="sample_attention",
    )(q, k_new, v_new, view(cache_k), view(cache_k), view(cache_v), view(cache_v), *tabs)


def _outproj_kernel(*refs, tm, merged):
    def project(cat_ref, x_ref, w_ref, x1_ref):
        for c0 in range(0, D_MODEL, NCHUNK):
            cs = slice(c0, c0 + NCHUNK)
            x1_ref[:, cs] = x_ref[:, cs] + jnp.dot(cat_ref[...], w_ref[:, cs],
                                                   preferred_element_type=F32)

    if merged:
        y_ref, a_ref, x_ref, w_ref, x1_ref, cat_scr = refs
        cat_scr[:, :D_CONV] = y_ref[...]
        cat_scr[:, D_CONV:] = a_ref[...].astype(BF16)
        project(cat_scr, x_ref, w_ref, x1_ref)
        return

    (y_ref, o0_ref, o1_ref, o2_ref, l0_ref, l1_ref, l2_ref, x_ref, w_ref,
     x1_ref, cat_a, cat_b, o_slab, l_slab) = refs

    def interleave(slab, src_ref, d, cols):
        src = lambda r: src_ref[r, :, cols].astype(F32)
        q = tm // 4
        if d == 4:
            for r in range(4):
                slab[0, pl.ds(r, q, stride=4), :] = src(r)
        else:
            assert d == 16
            for r4 in range(4):
                for s in range(4):
                    slab[1, pl.ds(r4 * q + s, tm // 16, stride=4), :] = src(r4 + 4 * s)
            for r4 in range(4):
                slab[0, pl.ds(r4, q, stride=4), :] = slab[1, r4 * q:(r4 + 1) * q, :]
        return slab[0]

    def merge_into(cat_ref):
        lse = [l0_ref[...]]
        for b, (src, (_, d)) in enumerate(zip((l1_ref, l2_ref), BRANCHES[1:])):
            lse.append(interleave(l_slab.at[b], src, d, slice(None)))
        mx = jnp.maximum(jnp.maximum(lse[0], lse[1]), lse[2])
        e = [jnp.exp2(v - mx) for v in lse]
        inv = 1.0 / (e[0] + e[1] + e[2])
        wgt = [v * inv for v in e]
        for h in range(N_HEADS):
            sl = slice(h * HEAD_DIM, (h + 1) * HEAD_DIM)
            att = wgt[0][:, h:h + 1] * o0_ref[:, sl].astype(F32)
            for b, (src, (_, d)) in enumerate(zip((o1_ref, o2_ref), BRANCHES[1:])):
                o = interleave(o_slab.at[(N_BRANCH - 1) * h + b], src, d, sl)
                att = att + wgt[b + 1][:, h:h + 1] * o
            cat_ref[:, D_CONV + h * HEAD_DIM:D_CONV + (h + 1) * HEAD_DIM] = att.astype(BF16)
        cat_ref[:, :D_CONV] = y_ref[...]

    i = pl.program_id(0)

    @pl.when(i == 0)
    def _():
        cat_b[...] = jnp.zeros_like(cat_b)

    for parity, (fill, use) in enumerate(((cat_a, cat_b), (cat_b, cat_a))):
        @pl.when(i % 2 == parity)
        def _(fill=fill, use=use):
            project(use, x_ref, w_ref, x1_ref)
            merge_into(fill)


def _outproj(y_conv, att, x, w_out):
    m = x.shape[0]
    merged = not isinstance(att, tuple)
    tm = min(TM_OUT, m)
    n_tiles = m // tm
    cat = pltpu.VMEM((tm, D_MODEL), BF16)
    if merged:
        fill_tile = use_tile = lambda i: i
        n_steps = n_tiles
        att_args, scratch = [att], [cat]
        att_specs = [pl.BlockSpec((tm, D_ATT), lambda i: (i, 0))]
    else:
        fill_tile = lambda i: jnp.minimum(i, n_tiles - 1)
        use_tile = lambda i: jnp.maximum(i - 1, 0)
        n_steps = n_tiles + 1
        cls = lambda d, cols: pl.BlockSpec((None if d == 1 else d, tm // d, cols),
                                           lambda i: (0, fill_tile(i), 0))
        att_args = list(att[0]) + list(att[1])
        att_specs = ([cls(d, D_ATT) for _, d in BRANCHES]
                     + [cls(d, HEAD_DIM) for _, d in BRANCHES])
        scratch = [cat, cat,
                   pltpu.VMEM(((N_BRANCH - 1) * N_HEADS, 2, tm, HEAD_DIM), F32),
                   pltpu.VMEM((N_BRANCH - 1, 2, tm, HEAD_DIM), F32)]
    x_rows = pl.BlockSpec((tm, D_MODEL), lambda i: (use_tile(i), 0))
    return pl.pallas_call(
        functools.partial(_outproj_kernel, tm=tm, merged=merged),
        grid=(n_steps,),
        in_specs=[pl.BlockSpec((tm, D_CONV), lambda i: (fill_tile(i), 0))] + att_specs
                 + [x_rows, _resident((D_MODEL, D_MODEL))],
        out_specs=x_rows,
        out_shape=jax.ShapeDtypeStruct((m, D_MODEL), F32),
        scratch_shapes=scratch, compiler_params=_cparams(1),
        name="outproj_sample" if merged else "outproj_prompt",
    )(y_conv, *att_args, x, w_out)


def _ffn_kernel(x1_ref, g_ref, wg_ref, wu_ref, wd_ref, o_ref, f_scr, *, tm):
    j = pl.program_id(1)

    @pl.when(j == 0)
    def _():
        def norm_rows(r, carry):
            rows = pl.ds(pl.multiple_of(r * NORM_ROWS, NORM_ROWS), NORM_ROWS)
            xs = x1_ref[rows, :]
            ms = jnp.mean(xs * xs, axis=-1, keepdims=True)
            f_scr[rows, :] = ((xs * lax.rsqrt(ms + EPS)) * g_ref[...]).astype(BF16)
            o_ref[rows, :] = xs
            return carry
        lax.fori_loop(0, tm // NORM_ROWS, norm_rows, 0, unroll=2)

    f = f_scr[...]
    gate = jnp.dot(f, wg_ref[...], preferred_element_type=F32)
    up = jnp.dot(f, wu_ref[...], preferred_element_type=F32)
    hidden = (gate * jax.nn.sigmoid(gate)) * up
    o_ref[...] += jnp.dot(hidden.astype(BF16), wd_ref[...], preferred_element_type=F32)


def _ffn(x1, gain, w_gate, w_up, w_down):
    m = x1.shape[0]
    tm = min(TM_FFN, m)
    tf = TF_FFN if m > tm else TF_FFN_SINGLE_TILE
    row = pl.BlockSpec((tm, D_MODEL), lambda i, j: (i, 0))
    return pl.pallas_call(
        functools.partial(_ffn_kernel, tm=tm),
        grid=(m // tm, D_FF // tf),
        in_specs=[row, pl.BlockSpec((1, D_MODEL), lambda i, j: (0, 0)),
                  pl.BlockSpec((D_MODEL, tf), lambda i, j: (0, j)),
                  pl.BlockSpec((D_MODEL, tf), lambda i, j: (0, j)),
                  pl.BlockSpec((tf, D_MODEL), lambda i, j: (j, 0))],
        out_specs=row,
        out_shape=jax.ShapeDtypeStruct((m, D_MODEL), F32),
        scratch_shapes=[pltpu.VMEM((tm, D_MODEL), BF16)],
        compiler_params=_cparams(2), name="ffn",
    )(x1, gain, w_gate, w_up, w_down)


def _conv_prev_rows(state, t):
    b, _, c = state.shape
    p1 = jnp.zeros((b, t, c), F32).at[:, 0].set(state[:, 1])
    p2 = jnp.zeros((b, t, c), F32).at[:, 0].set(state[:, 0]).at[:, 1].set(state[:, 1])
    return p1.reshape(b * t, c), p2.reshape(b * t, c)


def kernel(x_prompt, x_sample, cache_k, cache_v, state_conv, norm_mix, w_in, conv_w, q_norm,
           k_norm, rel_bias, w_out, norm_ffn, w_gate, w_up, w_down):
    depth = w_in.shape[0]
    bp, s, _ = x_prompt.shape
    bs, t, _ = x_sample.shape
    assert bp == 1 and t == SUBLANES and s % (max(TM_IN, QSTEP) * BRANCHES[-1][1]) == 0
    keep = min(MAX_WINDOW, s)

    band_bias = _band_bias_rows(rel_bias)

    xp = x_prompt.reshape(s, D_MODEL)
    xs = x_sample.reshape(bs * t, D_MODEL)
    outs = [[] for _ in range(6)]
    for l in range(depth):
        wi = w_in[l].astype(BF16)
        g_mix, g_ffn = norm_mix[l][None], norm_ffn[l][None]
        qg, kg = q_norm[l][None] * (LOG2_E * HEAD_DIM ** -0.5), k_norm[l][None]

        res = _inproj(xp, g_mix, wi, conv_w[l], qg, kg)
        y, qkv, (kf, vf, u_tail) = res[0], (res[1:4], res[4:7], res[7:10]), res[10:]
        att, (wo, wg, wu, wd) = _prompt_attention(qkv, band_bias,
                                                  (w_out[l], w_gate[l], w_up[l], w_down[l]))
        x1 = _outproj(y, att, xp, wo)
        xp = _ffn(x1, g_ffn, wg, wu, wd)
        outs[0].append(kf.reshape(1, keep, N_HEADS, HEAD_DIM))
        outs[1].append(vf.reshape(1, keep, N_HEADS, HEAD_DIM))
        outs[2].append(u_tail[u_tail.shape[0] - (CONV_WIDTH - 1):][None])

        prev = _conv_prev_rows(state_conv[l], t)
        y, q, k, v, kf, vf, u = _inproj(xs, g_mix, wi, conv_w[l], qg, kg, prev=prev)
        att = _sample_attention(q, k, v, cache_k[l], cache_v[l], rel_bias, t)
        x1 = _outproj(y, att, xs, wo)
        xs = _ffn(x1, g_ffn, wg, wu, wd)
        outs[3].append(kf.reshape(bs, t, N_HEADS, HEAD_DIM))
        outs[4].append(vf.reshape(bs, t, N_HEADS, HEAD_DIM))
        outs[5].append(u.reshape(bs, t, D_CONV)[:, t - (CONV_WIDTH - 1):])

    return (xp.reshape(1, s, D_MODEL), xs.reshape(bs, t, D_MODEL),
            jnp.stack(outs[0]), jnp.stack(outs[1]), jnp.stack(outs[2]),
            jnp.stack(outs[3]), jnp.stack(outs[4]), jnp.stack(outs[5]))
```

```python
import functools

import numpy as np
import jax
import jax.numpy as jnp
from jax import lax
from jax.experimental import pallas as pl
from jax.experimental.pallas import tpu as pltpu

D_MODEL = 2048
HEAD_DIM = 128
N_HEADS = 8
D_ATT = N_HEADS * HEAD_DIM
D_CONV = D_MODEL - D_ATT
CONV_WIDTH = 3
BRANCHES = ((128, 1), (512, 4), (2048, 16))
N_BRANCH = len(BRANCHES)
assert tuple(d for _, d in BRANCHES) == (1, 4, 16)
BAND = 128
MAX_WINDOW = 2048
N_BUCKETS = 32
REL_MAX_DISTANCE = 2048
D_FF = ((8 * D_MODEL + 767) // 768) * 256
D_IN = 3 * D_CONV + 3 * D_ATT
EPS = 1e-6
NEG = -1e30
LOG2_E = float(np.log2(np.e))

F32 = jnp.float32
BF16 = jnp.bfloat16

VMEM_LIMIT = 56 * 1024 * 1024
SUBLANES = 8
BF16_SUBLANES = 16

QB = 128
QSTEP = 256
SAMPLE_SEQS_PER_STEP = 2
TM_IN = 512
TM_OUT = 512
TM_FFN = 1024
TF_FFN = 512
TF_FFN_SINGLE_TILE = D_FF // 4
NCHUNK = 512
NORM_ROWS = 64
HEADS_PER_CHUNK = NCHUNK // HEAD_DIM
N_CONV_CHUNKS = D_CONV // NCHUNK
N_ATT_CHUNKS = D_ATT // NCHUNK
NT_DIMS = (((1,), (1,)), ((), ()))


def _bucket_of_distance(dist):
    dist = np.asarray(dist, np.int64)
    max_exact = N_BUCKETS // 2
    large = max_exact + (np.log(np.maximum(dist, 1) / max_exact)
                         / np.log(REL_MAX_DISTANCE / max_exact)
                         * (N_BUCKETS - max_exact)).astype(np.int64)
    large = np.minimum(large, N_BUCKETS - 1)
    return np.where(dist < max_exact, dist, large).astype(np.int32)


def _multiplicity(dist):
    dist = np.asarray(dist, np.int64)
    m = np.zeros(dist.shape, np.int64)
    for w, d in BRANCHES:
        m += ((dist >= 0) & (dist <= w) & (dist % d == 0)).astype(np.int64)
    return m


def _bias_at(rel_bias, dist):
    dist = np.asarray(dist)
    onehot = np.zeros((N_BUCKETS, dist.size), np.float32)
    onehot[_bucket_of_distance(dist).ravel(), np.arange(dist.size)] = 1.0
    vals = jnp.dot(rel_bias.astype(F32).T * LOG2_E, onehot, precision=lax.Precision.HIGHEST)
    return vals.reshape((N_HEADS,) + dist.shape)


def _cparams(n_axes):
    return pltpu.CompilerParams(dimension_semantics=("arbitrary",) * n_axes,
                                vmem_limit_bytes=VMEM_LIMIT)


def _resident(shape):
    nd = len(shape)
    return pl.BlockSpec(shape, lambda *_: (0,) * nd, pipeline_mode=pl.Buffered(1))


def _inproj_kernel(*refs, tm, sample):
    x_ref, g_ref, wa_ref, wb_ref, wc_ref, cw_ref, qg_ref, kg_ref = refs[:8]
    n_in, n_layouts = (10, 1) if sample else (8, N_BRANCH)
    if sample:
        p1_ref, p2_ref = refs[8:10]
    outs = list(refs[n_in:])
    y_ref = outs.pop(0)
    qkv_out = [[outs.pop(0) for _ in range(n_layouts)] for _ in range(3)]
    kf_ref, vf_ref, u_ref, h_scr, u_scr = outs[:5]
    if not sample:
        carry_scr, slab_scr = outs[5:]
    i = pl.program_id(0)
    j = pl.program_id(1)

    @pl.when(j == 0)
    def _():
        def norm_rows(r, carry):
            rows = pl.ds(pl.multiple_of(r * NORM_ROWS, NORM_ROWS), NORM_ROWS)
            xs = x_ref[rows, :]
            ms = jnp.mean(xs * xs, axis=-1, keepdims=True)
            h_scr[rows, :] = ((xs * lax.rsqrt(ms + EPS)) * g_ref[...]).astype(BF16)
            return carry
        lax.fori_loop(0, tm // NORM_ROWS, norm_rows, 0, unroll=2)

    if not sample:
        @pl.when((i == 0) & (j == 0))
        def _():
            carry_scr[...] = jnp.zeros_like(carry_scr)

    def proj(w_ref):
        return jnp.dot(h_scr[...], w_ref[...], preferred_element_type=F32)

    def conv_chunk(c):
        cs = slice(c * NCHUNK, (c + 1) * NCHUNK)
        u = proj(wc_ref) * proj(wa_ref)
        if sample:
            u_scr[0:SUBLANES, :] = jnp.zeros((SUBLANES, NCHUNK), F32)
        else:
            u_scr[0:SUBLANES, :] = carry_scr[:, cs]
        u_scr[SUBLANES:SUBLANES + tm, :] = u
        u_m1 = u_scr[SUBLANES - 1:SUBLANES - 1 + tm, :]
        u_m2 = u_scr[SUBLANES - 2:SUBLANES - 2 + tm, :]
        if sample:
            t_in_seq = lax.broadcasted_iota(jnp.int32, (tm, NCHUNK), 0) % SUBLANES
            u_m1 = jnp.where(t_in_seq == 0, p1_ref[...], u_m1)
            u_m2 = jnp.where(t_in_seq < 2, p2_ref[...], u_m2)
            u_ref[...] = u
        else:
            carry_scr[:, cs] = u[tm - SUBLANES:tm, :]
            u_ref[...] = u[tm - SUBLANES:tm, :]
        conv = u_m2 * cw_ref[0:1, :]
        conv = conv + u_m1 * cw_ref[1:2, :]
        conv = conv + u * cw_ref[2:3, :]
        y_ref[...] = (proj(wb_ref) * conv).astype(y_ref.dtype)

    def head_norm(z, gain):
        ms = jnp.mean(z * z, axis=-1, keepdims=True)
        return (z * lax.rsqrt(ms + EPS)) * gain

    def qkv_chunk(c):
        zq, zk, zv = proj(wa_ref), proj(wb_ref), proj(wc_ref)
        for hh in range(HEADS_PER_CHUNK):
            sl = slice(hh * HEAD_DIM, (hh + 1) * HEAD_DIM)
            kn = head_norm(zk[:, sl], kg_ref[...])
            vals = (head_norm(zq[:, sl], qg_ref[...]), kn, zv[:, sl])
            col = slice(c * NCHUNK + hh * HEAD_DIM, c * NCHUNK + (hh + 1) * HEAD_DIM)
            kf_ref[:, col] = kn
            vf_ref[:, col] = vals[2]
            for t_idx, (val, outs) in enumerate(zip(vals, qkv_out)):
                outs[0][:, sl] = val.astype(outs[0].dtype)
                if sample:
                    continue
                slab = slab_scr.at[t_idx * HEADS_PER_CHUNK + hh]
                slab[0] = val
                quarter = tm // 4
                for r4 in range(4):
                    cls4 = slab[0, pl.ds(r4, quarter, stride=4), :]
                    outs[1][r4, :, sl] = cls4.astype(outs[1].dtype)
                    slab[1, r4 * quarter:(r4 + 1) * quarter, :] = cls4
                for r4 in range(4):
                    for s in range(4):
                        cls16 = slab[1, pl.ds(r4 * quarter + s, tm // 16, stride=4), :]
                        outs[2][r4 + 4 * s, :, sl] = cls16.astype(outs[2].dtype)

    for c in range(N_CONV_CHUNKS):
        pl.when(j == c)(functools.partial(conv_chunk, c))
    for c in range(N_ATT_CHUNKS):
        pl.when(j == N_CONV_CHUNKS + c)(functools.partial(qkv_chunk, c))


def _inproj(x, gain, w_in, conv_w, q_gain, k_gain, prev=None):
    m = x.shape[0]
    sample = prev is not None
    tm = m if sample else TM_IN
    n_tiles = m // tm
    n_steps = N_CONV_CHUNKS + N_ATT_CHUNKS

    conv_col = lambda j: jnp.minimum(j, N_CONV_CHUNKS - 1)
    att_col = lambda j: jnp.maximum(j - N_CONV_CHUNKS, 0)
    def w_spec(t):
        def index(i, j):
            conv_blk = t * N_CONV_CHUNKS + j
            att_blk = 3 * N_CONV_CHUNKS + t * N_ATT_CHUNKS + (j - N_CONV_CHUNKS)
            return 0, jnp.where(j < N_CONV_CHUNKS, conv_blk, att_blk)
        return pl.BlockSpec((D_MODEL, NCHUNK), index)

    x_rows = pl.BlockSpec((tm, D_MODEL),
                          lambda i, j: (jnp.minimum(i + (j > 1).astype(jnp.int32), n_tiles - 1), 0))
    in_specs = [x_rows,
                pl.BlockSpec((1, D_MODEL), lambda i, j: (0, 0)),
                w_spec(0), w_spec(1), w_spec(2),
                pl.BlockSpec((CONV_WIDTH, NCHUNK), lambda i, j: (0, conv_col(j))),
                pl.BlockSpec((1, HEAD_DIM), lambda i, j: (0, 0)),
                pl.BlockSpec((1, HEAD_DIM), lambda i, j: (0, 0))]
    args = [x, gain, w_in, w_in, w_in, conv_w, q_gain, k_gain]
    conv_rows = pl.BlockSpec((tm, NCHUNK), lambda i, j: (i, conv_col(j)))
    att_rows = pl.BlockSpec((tm, NCHUNK), lambda i, j: (i, att_col(j)))
    win_rows = m if sample else min(MAX_WINDOW, m)
    win0 = n_tiles - win_rows // tm
    win = pl.BlockSpec((tm, D_ATT), lambda i, j: (jnp.maximum(i - win0, 0), 0))
    win_f32 = jax.ShapeDtypeStruct((win_rows, D_ATT), F32)
    scratch = [pltpu.VMEM((tm, D_MODEL), BF16), pltpu.VMEM((tm + SUBLANES, NCHUNK), F32)]
    if sample:
        in_specs += [conv_rows, conv_rows]
        args += list(prev)
        qkv_specs = [att_rows] * 3
        qkv_shapes = [jax.ShapeDtypeStruct((m, D_ATT), F32)] * 3
        u_spec, u_rows = conv_rows, m
    else:
        qkv_specs, qkv_shapes = [], []
        for _ in range(3):
            for _, d in BRANCHES:
                qkv_specs.append(pl.BlockSpec((None if d == 1 else d, tm // d, NCHUNK),
                                              lambda i, j: (0, i, att_col(j))))
                qkv_shapes.append(jax.ShapeDtypeStruct((d, m // d, D_ATT), BF16))
        u_spec = pl.BlockSpec((SUBLANES, NCHUNK), lambda i, j: (i, conv_col(j)))
        u_rows = n_tiles * SUBLANES
        scratch += [pltpu.VMEM((SUBLANES, D_CONV), F32),
                    pltpu.VMEM((3 * HEADS_PER_CHUNK, 2, tm, HEAD_DIM), F32)]
    out_specs = [conv_rows] + qkv_specs + [win, win, u_spec]
    out_shape = ([jax.ShapeDtypeStruct((m, D_CONV), BF16)] + qkv_shapes
                 + [win_f32, win_f32, jax.ShapeDtypeStruct((u_rows, D_CONV), F32)])
    return pl.pallas_call(
        functools.partial(_inproj_kernel, tm=tm, sample=sample),
        grid=(n_tiles, n_steps), in_specs=in_specs, out_specs=out_specs, out_shape=out_shape,
        scratch_shapes=scratch, compiler_params=_cparams(2),
        name="inproj_sample" if sample else "inproj_prompt",
    )(*args)


def _prompt_attn_kernel(*refs, steps_per_class, cast_every):
    n_cast = len(cast_every)
    base_ref = refs[0]
    ins = refs[1:1 + 5 * N_BRANCH]
    cast_in = refs[1 + 5 * N_BRANCH:1 + 5 * N_BRANCH + n_cast]
    outs = refs[1 + 5 * N_BRANCH + n_cast:-3]
    cast_out = outs[2 * N_BRANCH:]
    k_buf, v_buf, bias_ref = refs[-3:]
    i = pl.program_id(0)

    @pl.when(i == 0)
    def _():
        left = lax.broadcasted_iota(jnp.int32, (QB, 2 * QB), 1) < QB
        for h in range(N_HEADS):
            for g in range(N_BRANCH):
                row = jnp.broadcast_to(base_ref[h, g:g + 1, :], (QB, 2 * QB))
                tile = pltpu.roll(row, 0, 1, stride=1, stride_axis=0)
                bias_ref[h, 0, g] = tile
                bias_ref[h, 1, g] = jnp.where(left, NEG, tile)
    for src, dst, every in zip(cast_in, cast_out, cast_every):
        @pl.when(i % every == 0)
        def _(src=src, dst=dst):
            dst[...] = src[...].astype(dst.dtype)
    for g in range(N_BRANCH):
        _, kp_ref, kc_ref, vp_ref, vc_ref = ins[5 * g:5 * g + 5]
        for buf, p_ref, c_ref in ((k_buf, kp_ref, kc_ref), (v_buf, vp_ref, vc_ref)):
            buf[g, 0:QB, :] = p_ref[...]
            buf[g, QB:QB + QSTEP, :] = c_ref[...]
    lane = lax.broadcasted_iota(jnp.int32, (QB, HEAD_DIM), 1)

    def sub_block(j, carry):
        r0 = pl.multiple_of(j * QB, QB)
        for g in range(N_BRANCH):
            q_ref = ins[5 * g]
            o_ref, lse_ref = outs[2 * g:2 * g + 2]
            first = ((i % steps_per_class[g] == 0) & (j == 0)).astype(jnp.int32)
            lse_tile = jnp.zeros((QB, HEAD_DIM), F32)
            for h in range(N_HEADS):
                sl = slice(h * HEAD_DIM, (h + 1) * HEAD_DIM)
                k2 = k_buf[g, pl.ds(r0, 2 * QB), sl]
                v2 = v_buf[g, pl.ds(r0, 2 * QB), sl]
                s = lax.dot_general(q_ref[pl.ds(r0, QB), sl], k2, NT_DIMS,
                                    preferred_element_type=F32)
                s = s + bias_ref[h, first, g]
                m = jnp.max(s, axis=-1, keepdims=True)
                p = jnp.exp2(s - m)
                l = jnp.sum(p, axis=-1, keepdims=True)
                o = jnp.dot(p.astype(BF16), v2, preferred_element_type=F32)
                o_ref[pl.ds(r0, QB), sl] = (o / l).astype(o_ref.dtype)
                lse_tile = jnp.where(lane == h, m + jnp.log2(l), lse_tile)
            lse_ref[pl.ds(r0, QB), :] = lse_tile
        return carry

    lax.fori_loop(0, QSTEP // QB, sub_block, 0)


def _prompt_attention(qkv, band_bias, to_bf16):
    s = qkv[0][0].shape[1]
    n_blocks = s // QSTEP
    sub = QSTEP // QB
    in_specs = [_resident(band_bias.shape)]
    args = [band_bias]
    out_specs, out_shape, spc_all = [], [], []
    for g, (_, d) in enumerate(BRANCHES):
        spc = s // d // QSTEP
        spc_all.append(spc)
        cur = lambda i, spc=spc: (i // spc, i % spc, 0)
        prev = lambda i, spc=spc: (i // spc, jnp.maximum((i % spc) * sub - 1, 0), 0)
        blk = lambda f, cols=D_ATT: pl.BlockSpec((None, QB if f is prev else QSTEP, cols), f)
        in_specs += [blk(cur), blk(prev), blk(cur), blk(prev), blk(cur)]
        args += [qkv[0][g], qkv[1][g], qkv[1][g], qkv[2][g], qkv[2][g]]
        out_specs += [blk(cur), blk(cur, HEAD_DIM)]
        out_shape += [jax.ShapeDtypeStruct((d, s // d, D_ATT), BF16),
                      jax.ShapeDtypeStruct((d, s // d, HEAD_DIM), F32)]
    cast_every = []
    for w in to_bf16:
        rows = w.shape[0]
        every = next(e for e in (1, 2, 4, 8) if rows * e % (n_blocks * BF16_SUBLANES) == 0)
        cast_every.append(every)
        slab = pl.BlockSpec((rows * every // n_blocks, w.shape[1]),
                            lambda i, every=every: (i // every, 0))
        in_specs.append(slab)
        args.append(w)
        out_specs.append(slab)
        out_shape.append(jax.ShapeDtypeStruct(w.shape, BF16))
    outs = pl.pallas_call(
        functools.partial(_prompt_attn_kernel, steps_per_class=tuple(spc_all),
                          cast_every=tuple(cast_every)),
        grid=(n_blocks,), in_specs=in_specs, out_specs=out_specs, out_shape=out_shape,
        scratch_shapes=[pltpu.VMEM((N_BRANCH, QB + QSTEP, D_ATT), BF16)] * 2
                       + [pltpu.VMEM((N_HEADS, 2, N_BRANCH, QB, 2 * QB), F32)],
        compiler_params=_cparams(1), name="prompt_attention",
    )(*args)
    att = outs[:2 * N_BRANCH]
    return (att[0::2], att[1::2]), outs[2 * N_BRANCH:]


def _band_bias_rows(rel_bias):
    jj = np.arange(2 * QB)
    n = QB - jj
    inside = (n >= 0) & (n <= BAND)
    dist = np.stack([np.clip(n, 0, BAND) * d for _, d in BRANCHES])
    return jnp.where(inside, _bias_at(rel_bias, dist), NEG)


def _sample_attn_kernel(q_ref, kn_ref, vn_ref, ks_ref, kd_ref, vs_ref, vd_ref,
                        bs_ref, ms_ref, xs_ref, bd_ref, md_ref, xd_ref, bn_ref, mn_ref, xn_ref,
                        o_ref):
    n_seq = ks_ref.shape[0]
    t = q_ref.shape[0] // n_seq

    def logits(qk, b_ref, m_ref):
        n = qk.shape[-1]
        return ((qk.reshape(N_HEADS, t, n) + b_ref[...][None]).reshape(N_HEADS * t, n)
                + m_ref[...])

    for b in range(n_seq):
        rows = slice(b * t, (b + 1) * t)

        def head_major(ref):
            return jnp.concatenate([ref[rows, h * HEAD_DIM:(h + 1) * HEAD_DIM]
                                    for h in range(N_HEADS)], axis=0)

        def flat(ref):
            return ref[b].reshape(-1, HEAD_DIM).astype(BF16)

        q = head_major(q_ref)
        qb = q.astype(BF16)
        s = [logits(lax.dot_general(qb, flat(ks_ref), NT_DIMS, preferred_element_type=F32),
                    bs_ref, ms_ref),
             logits(lax.dot_general(qb, flat(kd_ref), NT_DIMS, preferred_element_type=F32),
                    bd_ref, md_ref),
             logits(lax.dot_general(q, head_major(kn_ref), NT_DIMS, preferred_element_type=F32),
                    bn_ref, mn_ref)]
        m = functools.reduce(jnp.maximum, [jnp.max(v, axis=-1, keepdims=True) for v in s])
        p = [jnp.exp2(v - m) * mult[...] for v, mult in zip(s, (xs_ref, xd_ref, xn_ref))]
        l = sum(jnp.sum(v, axis=-1, keepdims=True) for v in p)
        acc = jnp.dot(p[0].astype(BF16), flat(vs_ref), preferred_element_type=F32)
        acc = acc + jnp.dot(p[1].astype(BF16), flat(vd_ref), preferred_element_type=F32)
        acc = acc + jnp.dot(p[2], head_major(vn_ref), preferred_element_type=F32)
        acc = acc / l
        for h in range(N_HEADS):
            o_ref[rows, h * HEAD_DIM:(h + 1) * HEAD_DIM] = acc[h * t:(h + 1) * t, :]


def _sample_attention(q, k_new, v_new, cache_k, cache_v, rel_bias, t):
    b, w = cache_k.shape[:2]
    grp = BRANCHES[-1][1]
    dense = BRANCHES[-2][0]
    n_grp, sparse_grps, dense_grps = w // grp, (w - dense) // grp, dense // grp
    assert w % grp == 0 and dense % grp == 0 and t <= grp and sparse_grps % dense_grps == 0
    pos = np.arange(w).reshape(n_grp, grp)
    pos_sparse, pos_dense = pos[:sparse_grps, :t].ravel(), pos[sparse_grps:].ravel()
    qpos = w + np.arange(t)[:, None]
    assert not _multiplicity(qpos - pos[:sparse_grps, t:].reshape(1, -1)).any()

    def tables(dist, key_major):
        mult = _multiplicity(dist)
        bias = _bias_at(rel_bias, np.clip(dist, 0, REL_MAX_DISTANCE))
        ok = (mult > 0)[None, :, :, None] & np.eye(N_HEADS, dtype=bool)[:, None, None, :]
        mask = np.where(ok, 0.0, NEG).astype(np.float32)
        mult = np.where(ok, mult[None, :, :, None], 0).astype(np.float32)
        if key_major:
            bias = bias.transpose(1, 2, 0)
        else:
            bias = bias.transpose(1, 0, 2)
            mask, mult = mask.transpose(0, 1, 3, 2), mult.transpose(0, 1, 3, 2)
        rows = N_HEADS * t
        return (bias.reshape(t, -1), jnp.asarray(mask.reshape(rows, -1)),
                jnp.asarray(mult.reshape(rows, -1)))

    tabs = (tables(qpos - pos_sparse[None, :], True) + tables(qpos - pos_dense[None, :], True)
            + tables(qpos - qpos.T, False))

    view = lambda c: c.reshape(b, n_grp, grp * N_HEADS, HEAD_DIM)
    n_seq = SAMPLE_SEQS_PER_STEP
    assert b % n_seq == 0
    new = pl.BlockSpec((n_seq * t, D_ATT), lambda i: (i, 0))
    sparse = pl.BlockSpec((n_seq, sparse_grps, t * N_HEADS, HEAD_DIM), lambda i: (i, 0, 0, 0))
    dense_spec = pl.BlockSpec((n_seq, dense_grps, grp * N_HEADS, HEAD_DIM),
                              lambda i: (i, sparse_grps // dense_grps, 0, 0))
    return pl.pallas_call(
        _sample_attn_kernel,
        grid=(b // n_seq,),
        in_specs=[new, new, new, sparse, dense_spec, sparse, dense_spec]
                 + [_resident(a.shape) for a in tabs],
        out_specs=new,
        out_shape=jax.ShapeDtypeStruct((b * t, D_ATT), F32),
        compiler_params=_cparams(1), name="sample_attention",
    )(q, k_new, v_new, view(cache_k), view(cache_k), view(cache_v), view(cache_v), *tabs)


def _outproj_kernel(*refs, tm, merged):
    def project(cat_ref, x_ref, w_ref, x1_ref, r_ref):
        ss = jnp.zeros((tm, 1), F32)
        for c0 in range(0, D_MODEL, NCHUNK):
            cs = slice(c0, c0 + NCHUNK)
            x1 = x_ref[:, cs] + jnp.dot(cat_ref[...], w_ref[:, cs], preferred_element_type=F32)
            x1_ref[:, cs] = x1
            ss = ss + jnp.sum(x1 * x1, axis=-1, keepdims=True)
        r_ref[...] = jnp.broadcast_to(lax.rsqrt(ss * (1.0 / D_MODEL) + EPS), r_ref.shape)

    if merged:
        y_ref, a_ref, x_ref, w_ref, x1_ref, r_ref, cat_scr = refs
        cat_scr[:, :D_CONV] = y_ref[...]
        cat_scr[:, D_CONV:] = a_ref[...].astype(BF16)
        project(cat_scr, x_ref, w_ref, x1_ref, r_ref)
        return

    (y_ref, o0_ref, o1_ref, o2_ref, l0_ref, l1_ref, l2_ref, x_ref, w_ref,
     x1_ref, r_ref, cat_a, cat_b, o_slab, l_slab) = refs

    def interleave(slab, src_ref, d, cols):
        src = lambda r: src_ref[r, :, cols].astype(F32)
        q = tm // 4
        if d == 4:
            for r in range(4):
                slab[0, pl.ds(r, q, stride=4), :] = src(r)
        else:
            assert d == 16
            for r4 in range(4):
                for s in range(4):
                    slab[1, pl.ds(r4 * q + s, tm // 16, stride=4), :] = src(r4 + 4 * s)
            for r4 in range(4):
                slab[0, pl.ds(r4, q, stride=4), :] = slab[1, r4 * q:(r4 + 1) * q, :]
        return slab[0]

    def merge_into(cat_ref):
        lse = [l0_ref[...]]
        for b, (src, (_, d)) in enumerate(zip((l1_ref, l2_ref), BRANCHES[1:])):
            lse.append(interleave(l_slab.at[b], src, d, slice(None)))
        mx = jnp.maximum(jnp.maximum(lse[0], lse[1]), lse[2])
        e = [jnp.exp2(v - mx) for v in lse]
        inv = 1.0 / (e[0] + e[1] + e[2])
        wgt = [v * inv for v in e]
        for h in range(N_HEADS):
            sl = slice(h * HEAD_DIM, (h + 1) * HEAD_DIM)
            att = wgt[0][:, h:h + 1] * o0_ref[:, sl].astype(F32)
            for b, (src, (_, d)) in enumerate(zip((o1_ref, o2_ref), BRANCHES[1:])):
                o = interleave(o_slab.at[(N_BRANCH - 1) * h + b], src, d, sl)
                att = att + wgt[b + 1][:, h:h + 1] * o
            cat_ref[:, D_CONV + h * HEAD_DIM:D_CONV + (h + 1) * HEAD_DIM] = att.astype(BF16)
        cat_ref[:, :D_CONV] = y_ref[...]

    i = pl.program_id(0)

    @pl.when(i == 0)
    def _():
        cat_b[...] = jnp.zeros_like(cat_b)

    for parity, (fill, use) in enumerate(((cat_a, cat_b), (cat_b, cat_a))):
        @pl.when(i % 2 == parity)
        def _(fill=fill, use=use):
            project(use, x_ref, w_ref, x1_ref, r_ref)
            merge_into(fill)


def _outproj(y_conv, att, x, w_out):
    m = x.shape[0]
    merged = not isinstance(att, tuple)
    tm = min(TM_OUT, m)
    n_tiles = m // tm
    cat = pltpu.VMEM((tm, D_MODEL), BF16)
    if merged:
        fill_tile = use_tile = lambda i: i
        n_steps = n_tiles
        att_args, scratch = [att], [cat]
        att_specs = [pl.BlockSpec((tm, D_ATT), lambda i: (i, 0))]
    else:
        fill_tile = lambda i: jnp.minimum(i, n_tiles - 1)
        use_tile = lambda i: jnp.maximum(i - 1, 0)
        n_steps = n_tiles + 1
        cls = lambda d, cols: pl.BlockSpec((None if d == 1 else d, tm // d, cols),
                                           lambda i: (0, fill_tile(i), 0))
        att_args = list(att[0]) + list(att[1])
        att_specs = ([cls(d, D_ATT) for _, d in BRANCHES]
                     + [cls(d, HEAD_DIM) for _, d in BRANCHES])
        scratch = [cat, cat,
                   pltpu.VMEM(((N_BRANCH - 1) * N_HEADS, 2, tm, HEAD_DIM), F32),
                   pltpu.VMEM((N_BRANCH - 1, 2, tm, HEAD_DIM), F32)]
    x_rows = pl.BlockSpec((tm, D_MODEL), lambda i: (use_tile(i), 0))
    return pl.pallas_call(
        functools.partial(_outproj_kernel, tm=tm, merged=merged),
        grid=(n_steps,),
        in_specs=[pl.BlockSpec((tm, D_CONV), lambda i: (fill_tile(i), 0))] + att_specs
                 + [x_rows, _resident((D_MODEL, D_MODEL))],
        out_specs=[x_rows, pl.BlockSpec((tm, HEAD_DIM), lambda i: (use_tile(i), 0))],
        out_shape=[jax.ShapeDtypeStruct((m, D_MODEL), F32),
                   jax.ShapeDtypeStruct((m, HEAD_DIM), F32)],
        scratch_shapes=scratch, compiler_params=_cparams(1),
        name="outproj_sample" if merged else "outproj_prompt",
    )(y_conv, *att_args, x, w_out)


def _ffn_kernel(x1_ref, r_ref, g_ref, wg_ref, wu_ref, wd_ref, o_ref, f_scr, *, tm):
    j = pl.program_id(1)

    @pl.when(j == 0)
    def _():
        def norm_rows(r, carry):
            rows = pl.ds(pl.multiple_of(r * NORM_ROWS, NORM_ROWS), NORM_ROWS)
            xs = x1_ref[rows, :]
            f_scr[rows, :] = ((xs * r_ref[rows, 0:1]) * g_ref[...]).astype(BF16)
            o_ref[rows, :] = xs
            return carry
        lax.fori_loop(0, tm // NORM_ROWS, norm_rows, 0, unroll=2)

    f = f_scr[...]
    gate = jnp.dot(f, wg_ref[...], preferred_element_type=F32)
    up = jnp.dot(f, wu_ref[...], preferred_element_type=F32)
    hidden = (gate * jax.nn.sigmoid(gate)) * up
    o_ref[...] += jnp.dot(hidden.astype(BF16), wd_ref[...], preferred_element_type=F32)


def _ffn(x1, rstd, gain, w_gate, w_up, w_down):
    m = x1.shape[0]
    tm = min(TM_FFN, m)
    tf = TF_FFN if m > tm else TF_FFN_SINGLE_TILE
    row = pl.BlockSpec((tm, D_MODEL), lambda i, j: (i, 0))
    return pl.pallas_call(
        functools.partial(_ffn_kernel, tm=tm),
        grid=(m // tm, D_FF // tf),
        in_specs=[row, pl.BlockSpec((tm, HEAD_DIM), lambda i, j: (i, 0)),
                  pl.BlockSpec((1, D_MODEL), lambda i, j: (0, 0)),
                  pl.BlockSpec((D_MODEL, tf), lambda i, j: (0, j)),
                  pl.BlockSpec((D_MODEL, tf), lambda i, j: (0, j)),
                  pl.BlockSpec((tf, D_MODEL), lambda i, j: (j, 0))],
        out_specs=row,
        out_shape=jax.ShapeDtypeStruct((m, D_MODEL), F32),
        scratch_shapes=[pltpu.VMEM((tm, D_MODEL), BF16)],
        compiler_params=_cparams(2), name="ffn",
    )(x1, rstd, gain, w_gate, w_up, w_down)


def _conv_prev_rows(state, t):
    b, _, c = state.shape
    p1 = jnp.zeros((b, t, c), F32).at[:, 0].set(state[:, 1])
    p2 = jnp.zeros((b, t, c), F32).at[:, 0].set(state[:, 0]).at[:, 1].set(state[:, 1])
    return p1.reshape(b * t, c), p2.reshape(b * t, c)


def kernel(x_prompt, x_sample, cache_k, cache_v, state_conv, norm_mix, w_in, conv_w, q_norm,
           k_norm, rel_bias, w_out, norm_ffn, w_gate, w_up, w_down):
    depth = w_in.shape[0]
    bp, s, _ = x_prompt.shape
    bs, t, _ = x_sample.shape
    assert bp == 1 and t == SUBLANES and s % (max(TM_IN, QSTEP) * BRANCHES[-1][1]) == 0
    keep = min(MAX_WINDOW, s)

    band_bias = _band_bias_rows(rel_bias)

    xp = x_prompt.reshape(s, D_MODEL)
    xs = x_sample.reshape(bs * t, D_MODEL)
    outs = [[] for _ in range(6)]
    for l in range(depth):
        wi = w_in[l].astype(BF16)
        g_mix, g_ffn = norm_mix[l][None], norm_ffn[l][None]
        qg, kg = q_norm[l][None] * (LOG2_E * HEAD_DIM ** -0.5), k_norm[l][None]

        res = _inproj(xp, g_mix, wi, conv_w[l], qg, kg)
        y, qkv, (kf, vf, u_tail) = res[0], (res[1:4], res[4:7], res[7:10]), res[10:]
        att, (wo, wg, wu, wd) = _prompt_attention(qkv, band_bias,
                                                  (w_out[l], w_gate[l], w_up[l], w_down[l]))
        x1, rstd = _outproj(y, att, xp, wo)
        xp = _ffn(x1, rstd, g_ffn, wg, wu, wd)
        outs[0].append(kf.reshape(1, keep, N_HEADS, HEAD_DIM))
        outs[1].append(vf.reshape(1, keep, N_HEADS, HEAD_DIM))
        outs[2].append(u_tail[u_tail.shape[0] - (CONV_WIDTH - 1):][None])

        prev = _conv_prev_rows(state_conv[l], t)
        y, q, k, v, kf, vf, u = _inproj(xs, g_mix, wi, conv_w[l], qg, kg, prev=prev)
        att = _sample_attention(q, k, v, cache_k[l], cache_v[l], rel_bias, t)
        x1, rstd = _outproj(y, att, xs, wo)
        xs = _ffn(x1, rstd, g_ffn, wg, wu, wd)
        outs[3].append(kf.reshape(bs, t, N_HEADS, HEAD_DIM))
        outs[4].append(vf.reshape(bs, t, N_HEADS, HEAD_DIM))
        outs[5].append(u.reshape(bs, t, D_CONV)[:, t - (CONV_WIDTH - 1):])

    return (xp.reshape(1, s, D_MODEL), xs.reshape(bs, t, D_MODEL),
            jnp.stack(outs[0]), jnp.stack(outs[1]), jnp.stack(outs[2]),
            jnp.stack(outs[3]), jnp.stack(outs[4]), jnp.stack(outs[5]))
```

```python
import functools

import numpy as np
import jax
import jax.numpy as jnp
from jax import lax
from jax.experimental import pallas as pl
from jax.experimental.pallas import tpu as pltpu

D_MODEL = 2048
HEAD_DIM = 128
N_HEADS = 8
D_ATT = N_HEADS * HEAD_DIM
D_CONV = D_MODEL - D_ATT
CONV_WIDTH = 3
BRANCHES = ((128, 1), (512, 4), (2048, 16))
N_BRANCH = len(BRANCHES)
assert tuple(d for _, d in BRANCHES) == (1, 4, 16)
BAND = 128
MAX_WINDOW = 2048
N_BUCKETS = 32
REL_MAX_DISTANCE = 2048
D_FF = ((8 * D_MODEL + 767) // 768) * 256
D_IN = 3 * D_CONV + 3 * D_ATT
EPS = 1e-6
NEG = -1e30
LOG2_E = float(np.log2(np.e))

F32 = jnp.float32
BF16 = jnp.bfloat16

VMEM_LIMIT = 56 * 1024 * 1024
SUBLANES = 8
BF16_SUBLANES = 16

QB = 128
QSTEP = 256
SAMPLE_SEQS_PER_STEP = 2
TM_IN = 512
TM_OUT = 512
TM_FFN = 1024
TF_FFN = 512
TF_FFN_SINGLE_TILE = D_FF // 4
NCHUNK = 512
NORM_ROWS = 64
HEADS_PER_CHUNK = NCHUNK // HEAD_DIM
N_CONV_CHUNKS = D_CONV // NCHUNK
N_ATT_CHUNKS = D_ATT // NCHUNK
NT_DIMS = (((1,), (1,)), ((), ()))


def _bucket_of_distance(dist):
    dist = np.asarray(dist, np.int64)
    max_exact = N_BUCKETS // 2
    large = max_exact + (np.log(np.maximum(dist, 1) / max_exact)
                         / np.log(REL_MAX_DISTANCE / max_exact)
                         * (N_BUCKETS - max_exact)).astype(np.int64)
    large = np.minimum(large, N_BUCKETS - 1)
    return np.where(dist < max_exact, dist, large).astype(np.int32)


def _multiplicity(dist):
    dist = np.asarray(dist, np.int64)
    m = np.zeros(dist.shape, np.int64)
    for w, d in BRANCHES:
        m += ((dist >= 0) & (dist <= w) & (dist % d == 0)).astype(np.int64)
    return m


def _bias_at(rel_bias, dist):
    dist = np.asarray(dist)
    onehot = np.zeros((N_BUCKETS, dist.size), np.float32)
    onehot[_bucket_of_distance(dist).ravel(), np.arange(dist.size)] = 1.0
    vals = jnp.dot(rel_bias.astype(F32).T * LOG2_E, onehot, precision=lax.Precision.HIGHEST)
    return vals.reshape((N_HEADS,) + dist.shape)


def _cparams(n_axes):
    return pltpu.CompilerParams(dimension_semantics=("arbitrary",) * n_axes,
                                vmem_limit_bytes=VMEM_LIMIT)


def _resident(shape):
    nd = len(shape)
    return pl.BlockSpec(shape, lambda *_: (0,) * nd, pipeline_mode=pl.Buffered(1))


def _inproj_kernel(*refs, tm, sample):
    x_ref, g_ref, wa_ref, wb_ref, wc_ref, cw_ref, qg_ref, kg_ref = refs[:8]
    n_in, n_layouts = (10, 1) if sample else (8, N_BRANCH)
    if sample:
        p1_ref, p2_ref = refs[8:10]
    outs = list(refs[n_in:])
    y_ref = outs.pop(0)
    qkv_out = [[outs.pop(0) for _ in range(n_layouts)] for _ in range(3)]
    kf_ref, vf_ref, u_ref, h_scr, u_scr = outs[:5]
    if not sample:
        carry_scr, slab_scr = outs[5:]
    i = pl.program_id(0)
    j = pl.program_id(1)

    @pl.when(j == 0)
    def _():
        def norm_rows(r, carry):
            rows = pl.ds(pl.multiple_of(r * NORM_ROWS, NORM_ROWS), NORM_ROWS)
            xs = x_ref[rows, :]
            ms = jnp.mean(xs * xs, axis=-1, keepdims=True)
            h_scr[rows, :] = ((xs * lax.rsqrt(ms + EPS)) * g_ref[...]).astype(BF16)
            return carry
        lax.fori_loop(0, tm // NORM_ROWS, norm_rows, 0, unroll=2)

    if not sample:
        @pl.when((i == 0) & (j == 0))
        def _():
            carry_scr[...] = jnp.zeros_like(carry_scr)

    def proj(w_ref):
        return jnp.dot(h_scr[...], w_ref[...], preferred_element_type=F32)

    def conv_chunk(c):
        cs = slice(c * NCHUNK, (c + 1) * NCHUNK)
        u = proj(wc_ref) * proj(wa_ref)
        if sample:
            u_scr[0:SUBLANES, :] = jnp.zeros((SUBLANES, NCHUNK), F32)
        else:
            u_scr[0:SUBLANES, :] = carry_scr[:, cs]
        u_scr[SUBLANES:SUBLANES + tm, :] = u
        u_m1 = u_scr[SUBLANES - 1:SUBLANES - 1 + tm, :]
        u_m2 = u_scr[SUBLANES - 2:SUBLANES - 2 + tm, :]
        if sample:
            t_in_seq = lax.broadcasted_iota(jnp.int32, (tm, NCHUNK), 0) % SUBLANES
            u_m1 = jnp.where(t_in_seq == 0, p1_ref[...], u_m1)
            u_m2 = jnp.where(t_in_seq < 2, p2_ref[...], u_m2)
            u_ref[...] = u
        else:
            carry_scr[:, cs] = u[tm - SUBLANES:tm, :]
            u_ref[...] = u[tm - SUBLANES:tm, :]
        conv = u_m2 * cw_ref[0:1, :]
        conv = conv + u_m1 * cw_ref[1:2, :]
        conv = conv + u * cw_ref[2:3, :]
        y_ref[...] = (proj(wb_ref) * conv).astype(y_ref.dtype)

    def head_norm(z, gain):
        ms = jnp.mean(z * z, axis=-1, keepdims=True)
        return (z * lax.rsqrt(ms + EPS)) * gain

    def qkv_chunk(c):
        zq, zk, zv = proj(wa_ref), proj(wb_ref), proj(wc_ref)
        for hh in range(HEADS_PER_CHUNK):
            sl = slice(hh * HEAD_DIM, (hh + 1) * HEAD_DIM)
            kn = head_norm(zk[:, sl], kg_ref[...])
            vals = (head_norm(zq[:, sl], qg_ref[...]), kn, zv[:, sl])
            col = slice(c * NCHUNK + hh * HEAD_DIM, c * NCHUNK + (hh + 1) * HEAD_DIM)
            kf_ref[:, col] = kn
            vf_ref[:, col] = vals[2]
            for t_idx, (val, outs) in enumerate(zip(vals, qkv_out)):
                outs[0][:, sl] = val.astype(outs[0].dtype)
                if sample:
                    continue
                slab = slab_scr.at[t_idx * HEADS_PER_CHUNK + hh]
                slab[0] = val
                quarter = tm // 4
                for r4 in range(4):
                    cls4 = slab[0, pl.ds(r4, quarter, stride=4), :]
                    outs[1][r4, :, sl] = cls4.astype(outs[1].dtype)
                    slab[1, r4 * quarter:(r4 + 1) * quarter, :] = cls4
                for r4 in range(4):
                    for s in range(4):
                        cls16 = slab[1, pl.ds(r4 * quarter + s, tm // 16, stride=4), :]
                        outs[2][r4 + 4 * s, :, sl] = cls16.astype(outs[2].dtype)

    for c in range(N_CONV_CHUNKS):
        pl.when(j == c)(functools.partial(conv_chunk, c))
    for c in range(N_ATT_CHUNKS):
        pl.when(j == N_CONV_CHUNKS + c)(functools.partial(qkv_chunk, c))


def _inproj(x, gain, w_in, conv_w, q_gain, k_gain, prev=None):
    m = x.shape[0]
    sample = prev is not None
    tm = m if sample else TM_IN
    n_tiles = m // tm
    n_steps = N_CONV_CHUNKS + N_ATT_CHUNKS

    conv_col = lambda j: jnp.minimum(j, N_CONV_CHUNKS - 1)
    att_col = lambda j: jnp.maximum(j - N_CONV_CHUNKS, 0)
    def w_spec(t):
        def index(i, j):
            conv_blk = t * N_CONV_CHUNKS + j
            att_blk = 3 * N_CONV_CHUNKS + t * N_ATT_CHUNKS + (j - N_CONV_CHUNKS)
            return 0, jnp.where(j < N_CONV_CHUNKS, conv_blk, att_blk)
        return pl.BlockSpec((D_MODEL, NCHUNK), index)

    x_rows = pl.BlockSpec((tm, D_MODEL),
                          lambda i, j: (jnp.minimum(i + (j > 1).astype(jnp.int32), n_tiles - 1), 0))
    in_specs = [x_rows,
                pl.BlockSpec((1, D_MODEL), lambda i, j: (0, 0)),
                w_spec(0), w_spec(1), w_spec(2),
                pl.BlockSpec((CONV_WIDTH, NCHUNK), lambda i, j: (0, conv_col(j))),
                pl.BlockSpec((1, HEAD_DIM), lambda i, j: (0, 0)),
                pl.BlockSpec((1, HEAD_DIM), lambda i, j: (0, 0))]
    args = [x, gain, w_in, w_in, w_in, conv_w, q_gain, k_gain]
    conv_rows = pl.BlockSpec((tm, NCHUNK), lambda i, j: (i, conv_col(j)))
    att_rows = pl.BlockSpec((tm, NCHUNK), lambda i, j: (i, att_col(j)))
    win_rows = m if sample else min(MAX_WINDOW, m)
    win0 = n_tiles - win_rows // tm
    win = pl.BlockSpec((tm, D_ATT), lambda i, j: (jnp.maximum(i - win0, 0), 0))
    win_f32 = jax.ShapeDtypeStruct((win_rows, D_ATT), F32)
    scratch = [pltpu.VMEM((tm, D_MODEL), BF16), pltpu.VMEM((tm + SUBLANES, NCHUNK), F32)]
    if sample:
        in_specs += [conv_rows, conv_rows]
        args += list(prev)
        qkv_specs = [att_rows] * 3
        qkv_shapes = [jax.ShapeDtypeStruct((m, D_ATT), F32)] * 3
        u_spec, u_rows = conv_rows, m
    else:
        qkv_specs, qkv_shapes = [], []
        for _ in range(3):
            for _, d in BRANCHES:
                qkv_specs.append(pl.BlockSpec((None if d == 1 else d, tm // d, NCHUNK),
                                              lambda i, j: (0, i, att_col(j))))
                qkv_shapes.append(jax.ShapeDtypeStruct((d, m // d, D_ATT), BF16))
        u_spec = pl.BlockSpec((SUBLANES, NCHUNK), lambda i, j: (i, conv_col(j)))
        u_rows = n_tiles * SUBLANES
        scratch += [pltpu.VMEM((SUBLANES, D_CONV), F32),
                    pltpu.VMEM((3 * HEADS_PER_CHUNK, 2, tm, HEAD_DIM), F32)]
    out_specs = [conv_rows] + qkv_specs + [win, win, u_spec]
    out_shape = ([jax.ShapeDtypeStruct((m, D_CONV), BF16)] + qkv_shapes
                 + [win_f32, win_f32, jax.ShapeDtypeStruct((u_rows, D_CONV), F32)])
    return pl.pallas_call(
        functools.partial(_inproj_kernel, tm=tm, sample=sample),
        grid=(n_tiles, n_steps), in_specs=in_specs, out_specs=out_specs, out_shape=out_shape,
        scratch_shapes=scratch, compiler_params=_cparams(2),
        name="inproj_sample" if sample else "inproj_prompt",
    )(*args)


def _prompt_attn_kernel(*refs, steps_per_class, cast_every):
    n_cast = len(cast_every)
    base_ref = refs[0]
    ins = refs[1:1 + 5 * N_BRANCH]
    cast_in = refs[1 + 5 * N_BRANCH:1 + 5 * N_BRANCH + n_cast]
    outs = refs[1 + 5 * N_BRANCH + n_cast:-3]
    cast_out = outs[2 * N_BRANCH:]
    k_buf, v_buf, bias_ref = refs[-3:]
    i = pl.program_id(0)

    @pl.when(i == 0)
    def _():
        left = lax.broadcasted_iota(jnp.int32, (QB, 2 * QB), 1) < QB
        for h in range(N_HEADS):
            for g in range(N_BRANCH):
                row = jnp.broadcast_to(base_ref[h, g:g + 1, :], (QB, 2 * QB))
                tile = pltpu.roll(row, 0, 1, stride=1, stride_axis=0)
                bias_ref[h, 0, g] = tile
                bias_ref[h, 1, g] = jnp.where(left, NEG, tile)
    for src, dst, every in zip(cast_in, cast_out, cast_every):
        @pl.when(i % every == 0)
        def _(src=src, dst=dst):
            dst[...] = src[...].astype(dst.dtype)
    for g in range(N_BRANCH):
        _, kp_ref, kc_ref, vp_ref, vc_ref = ins[5 * g:5 * g + 5]
        for buf, p_ref, c_ref in ((k_buf, kp_ref, kc_ref), (v_buf, vp_ref, vc_ref)):
            buf[g, 0:QB, :] = p_ref[...]
            buf[g, QB:QB + QSTEP, :] = c_ref[...]
    lane = lax.broadcasted_iota(jnp.int32, (QB, HEAD_DIM), 1)

    def sub_block(j, carry):
        r0 = pl.multiple_of(j * QB, QB)
        for g in range(N_BRANCH):
            q_ref = ins[5 * g]
            o_ref, lse_ref = outs[2 * g:2 * g + 2]
            first = ((i % steps_per_class[g] == 0) & (j == 0)).astype(jnp.int32)
            lse_tile = jnp.zeros((QB, HEAD_DIM), F32)
            for h in range(N_HEADS):
                sl = slice(h * HEAD_DIM, (h + 1) * HEAD_DIM)
                k2 = k_buf[g, pl.ds(r0, 2 * QB), sl]
                v2 = v_buf[g, pl.ds(r0, 2 * QB), sl]
                s = lax.dot_general(q_ref[pl.ds(r0, QB), sl], k2, NT_DIMS,
                                    preferred_element_type=F32)
                s = s + bias_ref[h, first, g]
                m = jnp.max(s, axis=-1, keepdims=True)
                p = jnp.exp2(s - m)
                l = jnp.sum(p, axis=-1, keepdims=True)
                o = jnp.dot(p.astype(BF16), v2, preferred_element_type=F32)
                o_ref[pl.ds(r0, QB), sl] = (o / l).astype(o_ref.dtype)
                lse_tile = jnp.where(lane == h, m + jnp.log2(l), lse_tile)
            lse_ref[pl.ds(r0, QB), :] = lse_tile
        return carry

    lax.fori_loop(0, QSTEP // QB, sub_block, 0)


def _prompt_attention(qkv, band_bias, to_bf16):
    s = qkv[0][0].shape[1]
    n_blocks = s // QSTEP
    sub = QSTEP // QB
    in_specs = [_resident(band_bias.shape)]
    args = [band_bias]
    out_specs, out_shape, spc_all = [], [], []
    for g, (_, d) in enumerate(BRANCHES):
        spc = s // d // QSTEP
        spc_all.append(spc)
        cur = lambda i, spc=spc: (i // spc, i % spc, 0)
        prev = lambda i, spc=spc: (i // spc, jnp.maximum((i % spc) * sub - 1, 0), 0)
        blk = lambda f, cols=D_ATT: pl.BlockSpec((None, QB if f is prev else QSTEP, cols), f)
        in_specs += [blk(cur), blk(prev), blk(cur), blk(prev), blk(cur)]
        args += [qkv[0][g], qkv[1][g], qkv[1][g], qkv[2][g], qkv[2][g]]
        out_specs += [blk(cur), blk(cur, HEAD_DIM)]
        out_shape += [jax.ShapeDtypeStruct((d, s // d, D_ATT), BF16),
                      jax.ShapeDtypeStruct((d, s // d, HEAD_DIM), F32)]
    cast_every = []
    for w in to_bf16:
        rows = w.shape[0]
        every = next(e for e in (1, 2, 4, 8) if rows * e % (n_blocks * BF16_SUBLANES) == 0)
        cast_every.append(every)
        slab = pl.BlockSpec((rows * every // n_blocks, w.shape[1]),
                            lambda i, every=every: (i // every, 0))
        in_specs.append(slab)
        args.append(w)
        out_specs.append(slab)
        out_shape.append(jax.ShapeDtypeStruct(w.shape, BF16))
    outs = pl.pallas_call(
        functools.partial(_prompt_attn_kernel, steps_per_class=tuple(spc_all),
                          cast_every=tuple(cast_every)),
        grid=(n_blocks,), in_specs=in_specs, out_specs=out_specs, out_shape=out_shape,
        scratch_shapes=[pltpu.VMEM((N_BRANCH, QB + QSTEP, D_ATT), BF16)] * 2
                       + [pltpu.VMEM((N_HEADS, 2, N_BRANCH, QB, 2 * QB), F32)],
        compiler_params=_cparams(1), name="prompt_attention",
    )(*args)
    att = outs[:2 * N_BRANCH]
    return (att[0::2], att[1::2]), outs[2 * N_BRANCH:]


def _band_bias_rows(rel_bias):
    jj = np.arange(2 * QB)
    n = QB - jj
    inside = (n >= 0) & (n <= BAND)
    dist = np.stack([np.clip(n, 0, BAND) * d for _, d in BRANCHES])
    return jnp.where(inside, _bias_at(rel_bias, dist), NEG)


def _sample_attn_kernel(q_ref, kn_ref, vn_ref, ks_ref, kd_ref, vs_ref, vd_ref,
                        bs_ref, ms_ref, xs_ref, bd_ref, md_ref, xd_ref, bn_ref, mn_ref, xn_ref,
                        o_ref):
    n_seq = ks_ref.shape[0]
    t = q_ref.shape[0] // n_seq

    def logits(qk, b_ref, m_ref):
        n = qk.shape[-1]
        return ((qk.reshape(N_HEADS, t, n) + b_ref[...][None]).reshape(N_HEADS * t, n)
                + m_ref[...])

    for b in range(n_seq):
        rows = slice(b * t, (b + 1) * t)

        def head_major(ref):
            return jnp.concatenate([ref[rows, h * HEAD_DIM:(h + 1) * HEAD_DIM]
                                    for h in range(N_HEADS)], axis=0)

        def flat(ref):
            return ref[b].reshape(-1, HEAD_DIM).astype(BF16)

        q = head_major(q_ref)
        qb = q.astype(BF16)
        s = [logits(lax.dot_general(qb, flat(ks_ref), NT_DIMS, preferred_element_type=F32),
                    bs_ref, ms_ref),
             logits(lax.dot_general(qb, flat(kd_ref), NT_DIMS, preferred_element_type=F32),
                    bd_ref, md_ref),
             logits(lax.dot_general(q, head_major(kn_ref), NT_DIMS, preferred_element_type=F32),
                    bn_ref, mn_ref)]
        m = functools.reduce(jnp.maximum, [jnp.max(v, axis=-1, keepdims=True) for v in s])
        p = [jnp.exp2(v - m) * mult[...] for v, mult in zip(s, (xs_ref, xd_ref, xn_ref))]
        l = sum(jnp.sum(v, axis=-1, keepdims=True) for v in p)
        acc = jnp.dot(p[0].astype(BF16), flat(vs_ref), preferred_element_type=F32)
        acc = acc + jnp.dot(p[1].astype(BF16), flat(vd_ref), preferred_element_type=F32)
        acc = acc + jnp.dot(p[2], head_major(vn_ref), preferred_element_type=F32)
        acc = acc / l
        for h in range(N_HEADS):
            o_ref[rows, h * HEAD_DIM:(h + 1) * HEAD_DIM] = acc[h * t:(h + 1) * t, :]


def _sample_attention(q, k_new, v_new, cache_k, cache_v, rel_bias, t):
    b, w = cache_k.shape[:2]
    grp = BRANCHES[-1][1]
    dense = BRANCHES[-2][0]
    n_grp, sparse_grps, dense_grps = w // grp, (w - dense) // grp, dense // grp
    assert w % grp == 0 and dense % grp == 0 and t <= grp and sparse_grps % dense_grps == 0
    pos = np.arange(w).reshape(n_grp, grp)
    pos_sparse, pos_dense = pos[:sparse_grps, :t].ravel(), pos[sparse_grps:].ravel()
    qpos = w + np.arange(t)[:, None]
    assert not _multiplicity(qpos - pos[:sparse_grps, t:].reshape(1, -1)).any()

    dists = (qpos - pos_sparse[None, :], qpos - pos_dense[None, :], qpos - qpos.T)
    splits = np.cumsum([d.shape[1] for d in dists])[:-1].tolist()
    biases = jnp.split(_bias_at(rel_bias, np.clip(np.concatenate(dists, axis=1), 0,
                                                  REL_MAX_DISTANCE)), splits, axis=2)

    def tables(dist, bias, key_major):
        mult = _multiplicity(dist)
        ok = (mult > 0)[None, :, :, None] & np.eye(N_HEADS, dtype=bool)[:, None, None, :]
        mask = np.where(ok, 0.0, NEG).astype(np.float32)
        mult = np.where(ok, mult[None, :, :, None], 0).astype(np.float32)
        if key_major:
            bias = bias.transpose(1, 2, 0)
        else:
            bias = bias.transpose(1, 0, 2)
            mask, mult = mask.transpose(0, 1, 3, 2), mult.transpose(0, 1, 3, 2)
        rows = N_HEADS * t
        return (bias.reshape(t, -1), jnp.asarray(mask.reshape(rows, -1)),
                jnp.asarray(mult.reshape(rows, -1)))

    tabs = (tables(dists[0], biases[0], True) + tables(dists[1], biases[1], True)
            + tables(dists[2], biases[2], False))

    view = lambda c: c.reshape(b, n_grp, grp * N_HEADS, HEAD_DIM)
    n_seq = SAMPLE_SEQS_PER_STEP
    assert b % n_seq == 0
    new = pl.BlockSpec((n_seq * t, D_ATT), lambda i: (i, 0))
    sparse = pl.BlockSpec((n_seq, sparse_grps, t * N_HEADS, HEAD_DIM), lambda i: (i, 0, 0, 0))
    dense_spec = pl.BlockSpec((n_seq, dense_grps, grp * N_HEADS, HEAD_DIM),
                              lambda i: (i, sparse_grps // dense_grps, 0, 0))
    return pl.pallas_call(
        _sample_attn_kernel,
        grid=(b // n_seq,),
        in_specs=[new, new, new, sparse, dense_spec, sparse, dense_spec]
                 + [_resident(a.shape) for a in tabs],
        out_specs=new,
        out_shape=jax.ShapeDtypeStruct((b * t, D_ATT), F32),
        compiler_params=_cparams(1), name="sample_attention",
    )(q, k_new, v_new, view(cache_k), view(cache_k), view(cache_v), view(cache_v), *tabs)


def _outproj_kernel(*refs, tm, merged):
    def project(cat_ref, x_ref, w_ref, x1_ref, r_ref):
        ss = jnp.zeros((tm, 1), F32)
        for c0 in range(0, D_MODEL, NCHUNK):
            cs = slice(c0, c0 + NCHUNK)
            x1 = x_ref[:, cs] + jnp.dot(cat_ref[...], w_ref[:, cs], preferred_element_type=F32)
            x1_ref[:, cs] = x1
            ss = ss + jnp.sum(x1 * x1, axis=-1, keepdims=True)
        r_ref[...] = jnp.broadcast_to(lax.rsqrt(ss * (1.0 / D_MODEL) + EPS), r_ref.shape)

    if merged:
        y_ref, a_ref, x_ref, w_ref, x1_ref, r_ref, cat_scr = refs
        cat_scr[:, :D_CONV] = y_ref[...]
        cat_scr[:, D_CONV:] = a_ref[...].astype(BF16)
        project(cat_scr, x_ref, w_ref, x1_ref, r_ref)
        return

    (y_ref, o0_ref, o1_ref, o2_ref, l0_ref, l1_ref, l2_ref, x_ref, w_ref,
     x1_ref, r_ref, cat_a, cat_b, o_slab, l_slab) = refs

    def interleave(slab, src_ref, d, cols):
        src = lambda r: src_ref[r, :, cols].astype(F32)
        q = tm // 4
        if d == 4:
            for r in range(4):
                slab[0, pl.ds(r, q, stride=4), :] = src(r)
        else:
            assert d == 16
            for r4 in range(4):
                for s in range(4):
                    slab[1, pl.ds(r4 * q + s, tm // 16, stride=4), :] = src(r4 + 4 * s)
            for r4 in range(4):
                slab[0, pl.ds(r4, q, stride=4), :] = slab[1, r4 * q:(r4 + 1) * q, :]
        return slab[0]

    def merge_into(cat_ref):
        lse = [l0_ref[...]]
        for b, (src, (_, d)) in enumerate(zip((l1_ref, l2_ref), BRANCHES[1:])):
            lse.append(interleave(l_slab.at[b], src, d, slice(None)))
        mx = jnp.maximum(jnp.maximum(lse[0], lse[1]), lse[2])
        e = [jnp.exp2(v - mx) for v in lse]
        inv = 1.0 / (e[0] + e[1] + e[2])
        wgt = [v * inv for v in e]
        for h in range(N_HEADS):
            sl = slice(h * HEAD_DIM, (h + 1) * HEAD_DIM)
            att = wgt[0][:, h:h + 1] * o0_ref[:, sl].astype(F32)
            for b, (src, (_, d)) in enumerate(zip((o1_ref, o2_ref), BRANCHES[1:])):
                o = interleave(o_slab.at[(N_BRANCH - 1) * h + b], src, d, sl)
                att = att + wgt[b + 1][:, h:h + 1] * o
            cat_ref[:, D_CONV + h * HEAD_DIM:D_CONV + (h + 1) * HEAD_DIM] = att.astype(BF16)
        cat_ref[:, :D_CONV] = y_ref[...]

    i = pl.program_id(0)

    @pl.when(i == 0)
    def _():
        cat_b[...] = jnp.zeros_like(cat_b)

    for parity, (fill, use) in enumerate(((cat_a, cat_b), (cat_b, cat_a))):
        @pl.when(i % 2 == parity)
        def _(fill=fill, use=use):
            project(use, x_ref, w_ref, x1_ref, r_ref)
            merge_into(fill)


def _outproj(y_conv, att, x, w_out):
    m = x.shape[0]
    merged = not isinstance(att, tuple)
    tm = min(TM_OUT, m)
    n_tiles = m // tm
    cat = pltpu.VMEM((tm, D_MODEL), BF16)
    if merged:
        fill_tile = use_tile = lambda i: i
        n_steps = n_tiles
        att_args, scratch = [att], [cat]
        att_specs = [pl.BlockSpec((tm, D_ATT), lambda i: (i, 0))]
    else:
        fill_tile = lambda i: jnp.minimum(i, n_tiles - 1)
        use_tile = lambda i: jnp.maximum(i - 1, 0)
        n_steps = n_tiles + 1
        cls = lambda d, cols: pl.BlockSpec((None if d == 1 else d, tm // d, cols),
                                           lambda i: (0, fill_tile(i), 0))
        att_args = list(att[0]) + list(att[1])
        att_specs = ([cls(d, D_ATT) for _, d in BRANCHES]
                     + [cls(d, HEAD_DIM) for _, d in BRANCHES])
        scratch = [cat, cat,
                   pltpu.VMEM(((N_BRANCH - 1) * N_HEADS, 2, tm, HEAD_DIM), F32),
                   pltpu.VMEM((N_BRANCH - 1, 2, tm, HEAD_DIM), F32)]
    x_rows = pl.BlockSpec((tm, D_MODEL), lambda i: (use_tile(i), 0))
    return pl.pallas_call(
        functools.partial(_outproj_kernel, tm=tm, merged=merged),
        grid=(n_steps,),
        in_specs=[pl.BlockSpec((tm, D_CONV), lambda i: (fill_tile(i), 0))] + att_specs
                 + [x_rows, _resident((D_MODEL, D_MODEL))],
        out_specs=[x_rows, pl.BlockSpec((tm, HEAD_DIM), lambda i: (use_tile(i), 0))],
        out_shape=[jax.ShapeDtypeStruct((m, D_MODEL), F32),
                   jax.ShapeDtypeStruct((m, HEAD_DIM), F32)],
        scratch_shapes=scratch, compiler_params=_cparams(1),
        name="outproj_sample" if merged else "outproj_prompt",
    )(y_conv, *att_args, x, w_out)


def _ffn_kernel(x1_ref, r_ref, g_ref, wg_ref, wu_ref, wd_ref, o_ref, f_scr, *, tm):
    j = pl.program_id(1)

    @pl.when(j == 0)
    def _():
        def norm_rows(r, carry):
            rows = pl.ds(pl.multiple_of(r * NORM_ROWS, NORM_ROWS), NORM_ROWS)
            xs = x1_ref[rows, :]
            f_scr[rows, :] = ((xs * r_ref[rows, 0:1]) * g_ref[...]).astype(BF16)
            o_ref[rows, :] = xs
            return carry
        lax.fori_loop(0, tm // NORM_ROWS, norm_rows, 0, unroll=2)

    f = f_scr[...]
    gate = jnp.dot(f, wg_ref[...], preferred_element_type=F32)
    up = jnp.dot(f, wu_ref[...], preferred_element_type=F32)
    hidden = (gate * jax.nn.sigmoid(gate)) * up
    o_ref[...] += jnp.dot(hidden.astype(BF16), wd_ref[...], preferred_element_type=F32)


def _ffn(x1, rstd, gain, w_gate, w_up, w_down):
    m = x1.shape[0]
    tm = min(TM_FFN, m)
    tf = TF_FFN if m > tm else TF_FFN_SINGLE_TILE
    row = pl.BlockSpec((tm, D_MODEL), lambda i, j: (i, 0))
    return pl.pallas_call(
        functools.partial(_ffn_kernel, tm=tm),
        grid=(m // tm, D_FF // tf),
        in_specs=[row, pl.BlockSpec((tm, HEAD_DIM), lambda i, j: (i, 0)),
                  pl.BlockSpec((1, D_MODEL), lambda i, j: (0, 0)),
                  pl.BlockSpec((D_MODEL, tf), lambda i, j: (0, j)),
                  pl.BlockSpec((D_MODEL, tf), lambda i, j: (0, j)),
                  pl.BlockSpec((tf, D_MODEL), lambda i, j: (j, 0))],
        out_specs=row,
        out_shape=jax.ShapeDtypeStruct((m, D_MODEL), F32),
        scratch_shapes=[pltpu.VMEM((tm, D_MODEL), BF16)],
        compiler_params=_cparams(2), name="ffn",
    )(x1, rstd, gain, w_gate, w_up, w_down)


def _conv_prev_rows(state, t):
    b, _, c = state.shape
    p1 = jnp.zeros((b, t, c), F32).at[:, 0].set(state[:, 1])
    p2 = jnp.zeros((b, t, c), F32).at[:, 0].set(state[:, 0]).at[:, 1].set(state[:, 1])
    return p1.reshape(b * t, c), p2.reshape(b * t, c)


def kernel(x_prompt, x_sample, cache_k, cache_v, state_conv, norm_mix, w_in, conv_w, q_norm,
           k_norm, rel_bias, w_out, norm_ffn, w_gate, w_up, w_down):
    depth = w_in.shape[0]
    bp, s, _ = x_prompt.shape
    bs, t, _ = x_sample.shape
    assert bp == 1 and t == SUBLANES and s % (max(TM_IN, QSTEP) * BRANCHES[-1][1]) == 0
    keep = min(MAX_WINDOW, s)

    band_bias = _band_bias_rows(rel_bias)

    xp = x_prompt.reshape(s, D_MODEL)
    xs = x_sample.reshape(bs * t, D_MODEL)
    outs = [[] for _ in range(6)]
    for l in range(depth):
        wi = w_in[l].astype(BF16)
        g_mix, g_ffn = norm_mix[l][None], norm_ffn[l][None]
        qg, kg = q_norm[l][None] * (LOG2_E * HEAD_DIM ** -0.5), k_norm[l][None]

        res = _inproj(xp, g_mix, wi, conv_w[l], qg, kg)
        y, qkv, (kf, vf, u_tail) = res[0], (res[1:4], res[4:7], res[7:10]), res[10:]
        att, (wo, wg, wu, wd) = _prompt_attention(qkv, band_bias,
                                                  (w_out[l], w_gate[l], w_up[l], w_down[l]))
        x1, rstd = _outproj(y, att, xp, wo)
        xp = _ffn(x1, rstd, g_ffn, wg, wu, wd)
        outs[0].append(kf.reshape(1, keep, N_HEADS, HEAD_DIM))
        outs[1].append(vf.reshape(1, keep, N_HEADS, HEAD_DIM))
        outs[2].append(u_tail[u_tail.shape[0] - (CONV_WIDTH - 1):][None])

        prev = _conv_prev_rows(state_conv[l], t)
        y, q, k, v, kf, vf, u = _inproj(xs, g_mix, wi, conv_w[l], qg, kg, prev=prev)
        att = _sample_attention(q, k, v, cache_k[l], cache_v[l], rel_bias, t)
        x1, rstd = _outproj(y, att, xs, wo)
        xs = _ffn(x1, rstd, g_ffn, wg, wu, wd)
        outs[3].append(kf.reshape(bs, t, N_HEADS, HEAD_DIM))
        outs[4].append(vf.reshape(bs, t, N_HEADS, HEAD_DIM))
        outs[5].append(u.reshape(bs, t, D_CONV)[:, t - (CONV_WIDTH - 1):])

    return (xp.reshape(1, s, D_MODEL), xs.reshape(bs, t, D_MODEL),
            jnp.stack(outs[0]), jnp.stack(outs[1]), jnp.stack(outs[2]),
            jnp.stack(outs[3]), jnp.stack(outs[4]), jnp.stack(outs[5]))
```
